```python
import functools
import jax, jax.numpy as jnp
from jax import lax
import numpy as np

D_MODEL = 1024
BATCH = 2
SEQ = 8192
DEPTH = 2
DEC_BATCH = 128
DEC_SEQ = 4
PAST_LEN = 8192
PAGE_SIZE = 128

N_META = 16
D_MIX = D_MODEL
MLA_HEADS = 8
MLA_NOPE = 64
MLA_ROPE = 32
MLA_V = 64
MLA_QLORA = 384
MLA_KVLORA = 256
MLA_SCALE = (MLA_NOPE + MLA_ROPE) ** -0.5
Q_BLOCK = 128
NEG_BIG = -1e30
HG_HEADS = 4
HG_DK = 64
HG_DV = 64
HG_CHUNK = 64
RET_HEADS = 4
RET_DK = 64
RET_DV = 64
RET_CHUNK = 128
D_FF = 2816
N_EXPERTS = 8
TOP_K = 2
D_FF_EXPERT = 1408
ROPE_BASE = 10000.0
EPS = 1e-6
IN_SIZES = (MLA_QLORA, MLA_KVLORA, MLA_ROPE,
            HG_HEADS * HG_DK, HG_HEADS * HG_DK, HG_HEADS * HG_DV, HG_HEADS * HG_DV,
            RET_HEADS * RET_DK, RET_HEADS * RET_DK, RET_HEADS * RET_DV, RET_HEADS * RET_DV)
SPLIT_POINTS = tuple(int(s) for s in np.cumsum(IN_SIZES)[:-1])
N_IN = int(sum(IN_SIZES))

kernel_name = 'hybrid_mla_hgrn2_retention_step'


def rms_norm(x, g):
    xf = x.astype(jnp.float32)
    y = xf * lax.rsqrt(jnp.mean(xf * xf, axis=-1, keepdims=True) + EPS)
    return (y * g.astype(jnp.float32)).astype(x.dtype)


def rope(x, pos):
    d = x.shape[-1]
    inv = ROPE_BASE ** (-jnp.arange(0, d, 2, dtype=jnp.float32) / d)
    ang = pos.astype(jnp.float32)[:, None] * inv[None, :]
    ang = ang.reshape(ang.shape[:1] + (1,) * (x.ndim - 3) + ang.shape[1:])
    c, s = jnp.cos(ang), jnp.sin(ang)
    xf = x.astype(jnp.float32)
    x1, x2 = xf[..., : d // 2], xf[..., d // 2:]
    return jnp.concatenate([x1 * c - x2 * s, x2 * c + x1 * s], axis=-1).astype(x.dtype)


def retention_log_decay():
    return jnp.log1p(-jnp.exp2(-5.0 - jnp.arange(RET_HEADS, dtype=jnp.float32)))


def mixer_inputs(h, pos, lb, p):
    f32 = jnp.float32
    B, S = h.shape[:2]
    proj = h @ p['w_in']
    cq, ckv, kr, hq, hf, hi, hg, rq, rk, rv, rg = jnp.split(proj, SPLIT_POINTS, axis=-1)
    cq = rms_norm(cq, p['q_lora_g'])
    q = jnp.einsum('bsr,rhd->bshd', cq, p['w_uq'])
    q_nope = rms_norm(q[..., :MLA_NOPE], p['qn_nope_g'])
    q_rope = rope(rms_norm(q[..., MLA_NOPE:], p['qn_rope_g']), pos)
    mla_q = jnp.concatenate([q_nope, q_rope], axis=-1)
    ckv = rms_norm(ckv, p['kv_lora_g'])
    kr = rope(rms_norm(kr, p['kn_rope_g']), pos)
    hq = jax.nn.silu(hq.astype(f32)).reshape(B, S, HG_HEADS, HG_DK)
    z = hf.astype(f32).reshape(B, S, HG_HEADS, HG_DK)
    sig = jax.nn.sigmoid(z)
    f = lb + (1.0 - lb) * sig
    log_f = jnp.log(f)
    hk = 1.0 - f
    hv = hi.astype(f32).reshape(B, S, HG_HEADS, HG_DV)
    rq = rope(rq.reshape(B, S, RET_HEADS, RET_DK), pos).astype(f32)
    rk = rope(rk.reshape(B, S, RET_HEADS, RET_DK), pos).astype(f32) * (RET_DK ** -0.5)
    rv = rv.astype(f32).reshape(B, S, RET_HEADS, RET_DV)
    return (mla_q, ckv, kr), (hq, hk, hv, log_f, hg), (rq, rk, rv, rg)


def mla_keys_values(ckv, p):
    k_nope = rms_norm(jnp.einsum('bsr,rhd->bshd', ckv, p['w_uk']), p['kn_nope_g'])
    v = jnp.einsum('bsr,rhd->bshd', ckv, p['w_uv'])
    return k_nope, v


def mla_scores(q, k_nope, kr):
    s = jnp.einsum('bqhd,bthd->bhqt', q[..., :MLA_NOPE], k_nope) + jnp.einsum('bqhd,btd->bhqt', q[..., MLA_NOPE:], kr)
    return s.astype(jnp.float32) * MLA_SCALE


def attend_block(q, q_pos, k_nope, kr, v, k_pos):
    s = mla_scores(q, k_nope, kr)
    s = jnp.where(k_pos[None, None, None, :] <= q_pos[None, None, :, None], s, NEG_BIG)
    w = jax.nn.softmax(s, axis=-1)
    return jnp.einsum('bhqt,bthd->bqhd', w.astype(v.dtype), v)


def mla_prompt(q, ckv, kr, pos, p):
    B, T = q.shape[:2]
    k_nope, v = mla_keys_values(ckv, p)
    o_meta = attend_block(q[:, :N_META], pos[:N_META], k_nope[:, :N_META], kr[:, :N_META], v[:, :N_META], pos[:N_META])
    n_blk = (T - N_META) // Q_BLOCK
    q_blk = q[:, N_META:].reshape(B, n_blk, Q_BLOCK, MLA_HEADS, MLA_NOPE + MLA_ROPE).swapaxes(0, 1)
    pos_blk = pos[N_META:].reshape(n_blk, Q_BLOCK)
    o_blk = lax.map(lambda qp: attend_block(qp[0], qp[1], k_nope, kr, v, pos), (q_blk, pos_blk))
    o_real = o_blk.swapaxes(0, 1).reshape(B, T - N_META, MLA_HEADS, MLA_V)
    return jnp.concatenate([o_meta, o_real], axis=1).reshape(B, T, MLA_HEADS * MLA_V)


def mla_sample(q, ckv_new, kr_new, cache_ckv, cache_krope, layer, page_table, p):
    f32 = jnp.float32
    Bd, Sd = q.shape[:2]

    def page_step(carry, pages):
        m, l, acc = carry
        ckv_pg = cache_ckv[layer, pages]
        kr_pg = cache_krope[layer, pages]
        k_nope, v = mla_keys_values(ckv_pg.astype(q.dtype), p)
        s = mla_scores(q, k_nope, kr_pg.astype(q.dtype))
        m_new = jnp.maximum(m, jnp.max(s, axis=-1))
        corr = jnp.exp(m - m_new)
        e = jnp.exp(s - m_new[..., None])
        l = l * corr + jnp.sum(e, axis=-1)
        acc = acc * corr[..., None] + jnp.einsum('bhqt,bthd->bhqd', e, v.astype(f32))
        return (m_new, l, acc), None

    init = (jnp.full((Bd, MLA_HEADS, Sd), NEG_BIG, f32), jnp.zeros((Bd, MLA_HEADS, Sd), f32),
            jnp.zeros((Bd, MLA_HEADS, Sd, MLA_V), f32))
    (m, l, acc), _ = lax.scan(page_step, init, page_table.T)
    k_new, v_new = mla_keys_values(ckv_new, p)
    s = mla_scores(q, k_new, kr_new)
    s = jnp.where(jnp.tril(jnp.ones((Sd, Sd), bool)), s, NEG_BIG)
    m_f = jnp.maximum(m, jnp.max(s, axis=-1))
    corr = jnp.exp(m - m_f)
    e = jnp.exp(s - m_f[..., None])
    l = l * corr + jnp.sum(e, axis=-1)
    acc = acc * corr[..., None] + jnp.einsum('bhqt,bthd->bhqd', e, v_new.astype(f32))
    o = (acc / l[..., None]).transpose(0, 2, 1, 3).reshape(Bd, Sd, MLA_HEADS * MLA_V)
    return o.astype(q.dtype)


def hgrn_chunk(S, q, k, v, g):
    C = q.shape[1]
    b = jnp.cumsum(g, axis=1)
    causal = jnp.tril(jnp.ones((C, C), bool))[None, :, :, None, None]
    diff = b[:, :, None] - b[:, None, :]
    decay = jnp.where(causal, jnp.exp(jnp.where(causal, diff, 0.0)), 0.0)
    a = jnp.einsum('bthk,bshk,btshk->bhts', q, k, decay)
    o = jnp.einsum('bhts,bshv->bthv', a, v) + jnp.einsum('bthk,bhkv->bthv', q * jnp.exp(b), S)
    b_last = b[:, -1]
    S_new = jnp.exp(b_last)[..., None] * S + jnp.einsum('bshk,bshv->bhkv', k * jnp.exp(b_last[:, None] - b), v)
    return S_new, o


def retention_chunk(S, q, k, v, log_gamma):
    C = q.shape[1]
    idx = jnp.arange(C, dtype=jnp.float32)
    diff = idx[:, None] - idx[None, :]
    decay = jnp.where(diff >= 0, jnp.exp(jnp.maximum(diff, 0.0)[None] * log_gamma[:, None, None]), 0.0)
    a = jnp.einsum('bthd,bshd->bhts', q, k) * decay[None]
    inner = jnp.exp((idx + 1.0)[:, None] * log_gamma[None, :])
    o = jnp.einsum('bhts,bshv->bthv', a, v) + jnp.einsum('bthd,bhdv->bthv', q, S) * inner[None, :, :, None]
    k_dec = k * jnp.exp((C - 1.0 - idx)[:, None] * log_gamma[None, :])[None, :, :, None]
    S_new = jnp.exp(C * log_gamma)[None, :, None, None] * S + jnp.einsum('bshd,bshv->bhdv', k_dec, v)
    return S_new, o


def chunked_recurrence(chunk_fn, S0, inputs, chunk):
    S, o_head = chunk_fn(S0, *(a[:, :N_META] for a in inputs))
    B, T = inputs[0].shape[:2]
    n = (T - N_META) // chunk
    xs = tuple(a[:, N_META:].reshape((B, n, chunk) + a.shape[2:]).swapaxes(0, 1) for a in inputs)
    S, o_body = lax.scan(lambda s, x: chunk_fn(s, *x), S, xs)
    o_body = o_body.swapaxes(0, 1).reshape((B, T - N_META) + o_body.shape[3:])
    return S, jnp.concatenate([o_head, o_body], axis=1)


def merge_groups(o_mla, o_hg, hg_gate, o_ret, ret_gate, p):
    B, S = o_mla.shape[:2]
    dt = o_mla.dtype
    y_hg = rms_norm(o_hg, p['hg_norm_g']).reshape(B, S, -1) * jax.nn.silu(hg_gate.astype(jnp.float32))
    y_ret = rms_norm(o_ret, p['ret_norm_g']).reshape(B, S, -1) * jax.nn.silu(ret_gate.astype(jnp.float32))
    y = jnp.concatenate([o_mla, y_hg.astype(dt), y_ret.astype(dt)], axis=-1)
    return y @ p['w_o']


def swiglu(x, wg, wu, wd):
    return (jax.nn.silu(x @ wg) * (x @ wu)) @ wd


def moe_swiglu(x, w_router, wg, wu, wd):
    logits = (x @ w_router).astype(jnp.float32)
    top_v, top_i = lax.top_k(logits, TOP_K)
    w = jax.nn.softmax(top_v, axis=-1)
    gates = jnp.sum(jax.nn.one_hot(top_i, N_EXPERTS, dtype=jnp.float32) * w[..., None], axis=-2)
    y = jnp.zeros_like(x)
    for e in range(N_EXPERTS):
        y = y + gates[..., e:e + 1].astype(x.dtype) * swiglu(x, wg[e], wu[e], wd[e])
    return y


def setup_inputs(seed: int = 0) -> dict:
    key = jax.random.key(seed)
    ks = iter(jax.random.split(key, 48))
    f32 = jnp.float32

    def nrm(shape, scale):
        return jax.random.normal(next(ks), shape, f32) * scale

    def gain(shape):
        return 1.0 + nrm(shape, 0.02)

    n_pages = PAST_LEN // PAGE_SIZE
    n_pool = (5 * DEC_BATCH * n_pages + 3) // 4
    page_table = jax.random.permutation(next(ks), n_pool)[: DEC_BATCH * n_pages].reshape(DEC_BATCH, n_pages).astype(jnp.int32)
    n_dense = (DEPTH + 1) // 2
    n_moe = DEPTH // 2
    return {
        'x_prompt': nrm((BATCH, SEQ, D_MODEL), 1.0),
        'x_sample': nrm((DEC_BATCH, DEC_SEQ, D_MODEL), 1.0),
        'cache_ckv': nrm((DEPTH, n_pool, PAGE_SIZE, MLA_KVLORA), 1.0),
        'cache_krope': nrm((DEPTH, n_pool, PAGE_SIZE, MLA_ROPE), 1.0),
        'page_table': page_table,
        'state_hgrn': nrm((DEPTH, DEC_BATCH, HG_HEADS, HG_DK, HG_DV), 0.5),
        'state_ret': nrm((DEPTH, DEC_BATCH, RET_HEADS, RET_DK, RET_DV), 0.5),
        'meta_tokens': nrm((N_META, D_MODEL), 1.0),
        'norm_mix_g': gain((DEPTH, D_MODEL)),
        'w_in': nrm((DEPTH, D_MODEL, N_IN), D_MODEL ** -0.5),
        'q_lora_g': gain((DEPTH, MLA_QLORA)),
        'kv_lora_g': gain((DEPTH, MLA_KVLORA)),
        'w_uq': nrm((DEPTH, MLA_QLORA, MLA_HEADS, MLA_NOPE + MLA_ROPE), MLA_QLORA ** -0.5),
        'w_uk': nrm((DEPTH, MLA_KVLORA, MLA_HEADS, MLA_NOPE), MLA_KVLORA ** -0.5),
        'w_uv': nrm((DEPTH, MLA_KVLORA, MLA_HEADS, MLA_V), MLA_KVLORA ** -0.5),
        'qn_nope_g': gain((DEPTH, MLA_NOPE)),
        'qn_rope_g': gain((DEPTH, MLA_ROPE)),
        'kn_nope_g': gain((DEPTH, MLA_NOPE)),
        'kn_rope_g': gain((DEPTH, MLA_ROPE)),
        'hg_lb': nrm((DEPTH, HG_HEADS * HG_DK), 1.0),
        'hg_norm_g': gain((DEPTH, HG_DV)),
        'ret_norm_g': gain((DEPTH, RET_DV)),
        'w_o': nrm((DEPTH, D_MIX, D_MODEL), D_MIX ** -0.5),
        'norm_ffn_g': gain((DEPTH, D_MODEL)),
        'ffn_w_gate': nrm((n_dense, D_MODEL, D_FF), D_MODEL ** -0.5),
        'ffn_w_up': nrm((n_dense, D_MODEL, D_FF), D_MODEL ** -0.5),
        'ffn_w_down': nrm((n_dense, D_FF, D_MODEL), D_FF ** -0.5),
        'moe_router': nrm((n_moe, D_MODEL, N_EXPERTS), D_MODEL ** -0.5),
        'moe_w_gate': nrm((n_moe, N_EXPERTS, D_MODEL, D_FF_EXPERT), D_MODEL ** -0.5),
        'moe_w_up': nrm((n_moe, N_EXPERTS, D_MODEL, D_FF_EXPERT), D_MODEL ** -0.5),
        'moe_w_down': nrm((n_moe, N_EXPERTS, D_FF_EXPERT, D_MODEL), D_FF_EXPERT ** -0.5),
    }


def reference(x_prompt, x_sample, cache_ckv, cache_krope, page_table, state_hgrn, state_ret, meta_tokens,
              norm_mix_g, w_in, q_lora_g, kv_lora_g, w_uq, w_uk, w_uv, qn_nope_g, qn_rope_g, kn_nope_g, kn_rope_g,
              hg_lb, hg_norm_g, ret_norm_g, w_o, norm_ffn_g, ffn_w_gate, ffn_w_up, ffn_w_down,
              moe_router, moe_w_gate, moe_w_up, moe_w_down):
    f32 = jnp.float32
    dt = x_prompt.dtype
    B = x_prompt.shape[0]
    Sd = x_sample.shape[1]
    past_len = page_table.shape[1] * cache_ckv.shape[2]
    h_p = jnp.concatenate([jnp.broadcast_to(meta_tokens.astype(dt)[None], (B, N_META, D_MODEL)), x_prompt], axis=1)
    h_s = x_sample
    pos_p = jnp.arange(h_p.shape[1], dtype=f32)
    pos_s = jnp.arange(Sd, dtype=f32) + past_len
    lb_soft = jax.nn.softmax(hg_lb.astype(f32), axis=0)
    lb_all = jnp.cumsum(lb_soft, axis=0) - lb_soft[0]
    log_gamma = retention_log_decay()
    ret_fn = functools.partial(retention_chunk, log_gamma=log_gamma)

    def channel_mixer(x, l):
        if l % 2 == 0:
            i = l // 2
            return swiglu(x, ffn_w_gate[i], ffn_w_up[i], ffn_w_down[i])
        i = l // 2
        return moe_swiglu(x, moe_router[i], moe_w_gate[i], moe_w_up[i], moe_w_down[i])

    ckv_p, kr_p, hg_p, ret_p = [], [], [], []
    ckv_s, kr_s, hg_s, ret_s = [], [], [], []
    for l in range(DEPTH):
        p = {'w_in': w_in[l], 'q_lora_g': q_lora_g[l], 'kv_lora_g': kv_lora_g[l], 'w_uq': w_uq[l],
             'w_uk': w_uk[l], 'w_uv': w_uv[l], 'qn_nope_g': qn_nope_g[l], 'qn_rope_g': qn_rope_g[l],
             'kn_nope_g': kn_nope_g[l], 'kn_rope_g': kn_rope_g[l], 'hg_norm_g': hg_norm_g[l],
             'ret_norm_g': ret_norm_g[l], 'w_o': w_o[l]}
        lb = lb_all[l].reshape(HG_HEADS, HG_DK)

        (mq, ckv, kr), hgi, reti = mixer_inputs(rms_norm(h_p, norm_mix_g[l]), pos_p, lb, p)
        o_mla = mla_prompt(mq, ckv, kr, pos_p, p)
        S_hg, o_hg = chunked_recurrence(hgrn_chunk, jnp.zeros((B, HG_HEADS, HG_DK, HG_DV), f32), hgi[:4], HG_CHUNK)
        S_rt, o_rt = chunked_recurrence(ret_fn, jnp.zeros((B, RET_HEADS, RET_DK, RET_DV), f32), reti[:3], RET_CHUNK)
        h_p = h_p + merge_groups(o_mla, o_hg, hgi[4], o_rt, reti[3], p)
        h_p = h_p + channel_mixer(rms_norm(h_p, norm_ffn_g[l]), l)
        ckv_p.append(ckv)
        kr_p.append(kr)
        hg_p.append(S_hg.astype(state_hgrn.dtype))
        ret_p.append(S_rt.astype(state_ret.dtype))

        (mq, ckv, kr), hgi, reti = mixer_inputs(rms_norm(h_s, norm_mix_g[l]), pos_s, lb, p)
        o_mla = mla_sample(mq, ckv, kr, cache_ckv, cache_krope, l, page_table, p)
        S_hg, o_hg = hgrn_chunk(state_hgrn[l].astype(f32), *hgi[:4])
        S_rt, o_rt = ret_fn(state_ret[l].astype(f32), *reti[:3])
        h_s = h_s + merge_groups(o_mla, o_hg, hgi[4], o_rt, reti[3], p)
        h_s = h_s + channel_mixer(rms_norm(h_s, norm_ffn_g[l]), l)
        ckv_s.append(ckv)
        kr_s.append(kr)
        hg_s.append(S_hg.astype(state_hgrn.dtype))
        ret_s.append(S_rt.astype(state_ret.dtype))

    y_prompt = h_p[:, N_META:]
    y_sample = h_s
    return (y_prompt, y_sample, jnp.stack(ckv_p), jnp.stack(kr_p), jnp.stack(hg_p), jnp.stack(ret_p),
            jnp.stack(ckv_s), jnp.stack(kr_s), jnp.stack(hg_s), jnp.stack(ret_s))
```

```python
import functools

import numpy as np
import jax
import jax.numpy as jnp
from jax import lax
from jax.experimental import pallas as pl
from jax.experimental.pallas import tpu as pltpu

F32 = jnp.float32
BF16 = jnp.bfloat16

EPS = 1e-6
ROPE_BASE = 10000.0
NEG_BIG = -1e30
LANES = 128
VMEM_LIMIT = 56 * 1024 * 1024

HG_CHUNK = 64
RET_CHUNK = 128
SMALL_CHUNK = 16
PAGES_PER_STEP = 8


def _params(*sem):
    return pltpu.CompilerParams(dimension_semantics=sem, vmem_limit_bytes=VMEM_LIMIT)


def _row_tile(n, pref):
    if n <= pref:
        return n
    t = pref - pref % 16
    while t >= 16:
        if n % t == 0:
            return t
        t -= 16
    return n


def _dot(a, b):
    return jnp.dot(a.astype(BF16), b.astype(BF16), preferred_element_type=F32)


def _dot_nt(a, b):
    return lax.dot_general(a.astype(BF16), b.astype(BF16), (((1,), (1,)), ((), ())), preferred_element_type=F32)


def _dot_tn(a, b):
    return lax.dot_general(a.astype(BF16), b.astype(BF16), (((0,), (0,)), ((), ())), preferred_element_type=F32)


def _rms(x, width):
    return x * lax.rsqrt(jnp.sum(x * x, axis=-1, keepdims=True) * (1.0 / width) + EPS)


def _rope128(y, cos, sin_signed, half):
    lane = lax.broadcasted_iota(jnp.int32, y.shape, 1)
    first = (lane % (2 * half)) < half
    rot = jnp.where(first, pltpu.roll(y, LANES - half, 1), pltpu.roll(y, half, 1))
    return y * cos + rot * sin_signed


def _sigmoid(x):
    return 1.0 / (1.0 + jnp.exp(-x))


def _inproj_kernel(h_ref, gn_ref, w_ref, gq_ref, gkv_ref, gkr_ref, lb_ref, cm_ref, sm_ref, cr_ref, sr_ref,
                   cq_ref, ckv_ref, kr_ref, hg_ref, rt_ref, *, nq, nkv, rope_m, dh):
    x = h_ref[...]
    xb = (_rms(x, x.shape[-1]) * gn_ref[...]).astype(BF16)

    def proj(lo, width):
        return jnp.dot(xb, w_ref[:, lo:lo + width], preferred_element_type=F32)

    c = proj(0, nq)
    cq_ref[...] = (_rms(c, nq) * gq_ref[...]).astype(BF16)
    c = proj(nq, nkv)
    ckv_ref[...] = _rms(c, nkv) * gkv_ref[...]
    k = proj(nq + nkv, LANES)
    k = _rms(k, rope_m) * gkr_ref[...]
    kr_ref[...] = _rope128(k, cm_ref[...], sm_ref[...], rope_m // 2)

    base = nq + nkv + LANES
    hq = proj(base, dh)
    hg_ref[:, 0:dh] = hq * _sigmoid(hq)
    z = proj(base + dh, dh)
    lb = lb_ref[...]
    f = lb + (1.0 - lb) * _sigmoid(z)
    hg_ref[:, dh:2 * dh] = 1.0 - f
    hg_ref[:, 2 * dh:3 * dh] = jnp.log(f)
    hg_ref[:, 3 * dh:4 * dh] = proj(base + 2 * dh, dh)
    hg_ref[:, 4 * dh:5 * dh] = proj(base + 3 * dh, dh)

    base = base + 4 * dh
    cr = cr_ref[...]
    sr = sr_ref[...]
    for part, scale in ((0, 1.0), (1, 64 ** -0.5)):
        for s in range(dh // LANES):
            lo = part * dh + s * LANES
            y = _rope128(proj(base + lo, LANES), cr, sr, 32)
            rt_ref[:, lo:lo + LANES] = y * scale
    rt_ref[:, 2 * dh:3 * dh] = proj(base + 2 * dh, dh)
    rt_ref[:, 3 * dh:4 * dh] = proj(base + 3 * dh, dh)


def _inproj(h, gn, w, gq, gkv, gkr, lb, tabs, *, nq, nkv, rope_m, dh):
    n, d = h.shape
    tm = _row_tile(n, 512)
    cm, sm, cr, sr = tabs
    row = lambda width: pl.BlockSpec((tm, width), lambda i: (i, 0))
    const = lambda a: pl.BlockSpec(a.shape, lambda i: (0, 0))
    return pl.pallas_call(
        functools.partial(_inproj_kernel, nq=nq, nkv=nkv, rope_m=rope_m, dh=dh),
        grid=(n // tm,),
        in_specs=[row(d), const(gn), const(w), const(gq), const(gkv), const(gkr), const(lb),
                  row(LANES), row(LANES), row(LANES), row(LANES)],
        out_specs=[row(nq), row(nkv), row(LANES), row(5 * dh), row(4 * dh)],
        out_shape=[jax.ShapeDtypeStruct((n, nq), BF16), jax.ShapeDtypeStruct((n, nkv), F32),
                   jax.ShapeDtypeStruct((n, LANES), F32), jax.ShapeDtypeStruct((n, 5 * dh), F32),
                   jax.ShapeDtypeStruct((n, 4 * dh), F32)],
        compiler_params=_params("parallel"),
        name="inproj",
    )(h, gn, w, gq, gkv, gkr, lb, cm, sm, cr, sr)


def _qprep_kernel(c_ref, w_ref, g_ref, gk_ref, cm_ref, sm_ref, *out_refs, nh, nope, rope_m, scale, with_qg):
    c = c_ref[...]
    cm = cm_ref[...]
    sm = sm_ref[...]
    g = g_ref[...]
    for h in range(nh):
        x = jnp.dot(c, w_ref[:, h * LANES:(h + 1) * LANES], preferred_element_type=F32)
        lane = lax.broadcasted_iota(jnp.int32, x.shape, 1)
        is_nope = lane < nope
        x2 = x * x
        ss_n = jnp.sum(jnp.where(is_nope, x2, 0.0), axis=-1, keepdims=True)
        ss_r = jnp.sum(jnp.where(is_nope, 0.0, x2), axis=-1, keepdims=True)
        inv = jnp.where(is_nope, lax.rsqrt(ss_n * (1.0 / nope) + EPS), lax.rsqrt(ss_r * (1.0 / rope_m) + EPS))
        y = _rope128(x * inv * g, cm, sm, rope_m // 2) * scale
        out_refs[0][:, h * LANES:(h + 1) * LANES] = y.astype(BF16)
        if with_qg:
            out_refs[1][:, h * LANES:(h + 1) * LANES] = (y * gk_ref[...]).astype(BF16)


def _qprep(cqn, w, g, gk, cm, sm, *, nh, nope, rope_m, scale, with_qg):
    n, r = cqn.shape
    tm = _row_tile(n, 512)
    row = lambda width: pl.BlockSpec((tm, width), lambda i: (i, 0))
    const = lambda a: pl.BlockSpec(a.shape, lambda i: (0, 0))
    nout = 2 if with_qg else 1
    return pl.pallas_call(
        functools.partial(_qprep_kernel, nh=nh, nope=nope, rope_m=rope_m, scale=scale, with_qg=with_qg),
        grid=(n // tm,),
        in_specs=[row(r), const(w), const(g), const(gk), row(LANES), row(LANES)],
        out_specs=[row(nh * LANES)] * nout,
        out_shape=[jax.ShapeDtypeStruct((n, nh * LANES), BF16)] * nout,
        compiler_params=_params("parallel"),
        name="qprep",
    )(cqn, w, g, gk, cm, sm)


def _kvup_kernel(c_ref, kr_ref, wk_ref, wv_ref, gk_ref, k_ref, v_ref, *, nh, nope):
    c = c_ref[...].astype(BF16)
    kr = kr_ref[...]
    gk = gk_ref[...]
    for h in range(nh):
        x = jnp.dot(c, wk_ref[:, h * LANES:(h + 1) * LANES], preferred_element_type=F32)
        k_ref[:, h * LANES:(h + 1) * LANES] = (_rms(x, nope) * gk + kr).astype(BF16)
    v_ref[...] = jnp.dot(c, wv_ref[...], preferred_element_type=F32).astype(BF16)


def _kvup(ckv, krs, wk, wv, gk, *, nh, nope):
    n, r = ckv.shape
    tm = _row_tile(n, 512)
    row = lambda width: pl.BlockSpec((tm, width), lambda i: (i, 0))
    const = lambda a: pl.BlockSpec(a.shape, lambda i: (0, 0))
    nv = wv.shape[1]
    return pl.pallas_call(
        functools.partial(_kvup_kernel, nh=nh, nope=nope),
        grid=(n // tm,),
        in_specs=[row(r), row(LANES), const(wk), const(wv), const(gk)],
        out_specs=[row(nh * LANES), row(nv)],
        out_shape=[jax.ShapeDtypeStruct((n, nh * LANES), BF16), jax.ShapeDtypeStruct((n, nv), BF16)],
        compiler_params=_params("parallel"),
        name="kvup",
    )(ckv, krs, wk, wv, gk)


def _flash_kernel(*refs, nh, has_prefix):
    if has_prefix:
        q_ref, k_ref, v_ref, kp_ref, vp_ref, o_ref, m_sc, l_sc, acc_sc = refs
    else:
        q_ref, k_ref, v_ref, o_ref, m_sc, l_sc, acc_sc = refs
    qi = pl.program_id(1)
    ki = pl.program_id(2)
    tq = q_ref.shape[1]

    def update(kblk, vblk, masked):
        for p in range(nh // 2):
            v = vblk(p)
            alphas, pvs = [], []
            for h in (2 * p, 2 * p + 1):
                q = q_ref[0, :, h * LANES:(h + 1) * LANES]
                s = _dot_nt(q, kblk(h))
                if masked:
                    row = lax.broadcasted_iota(jnp.int32, s.shape, 0)
                    col = lax.broadcasted_iota(jnp.int32, s.shape, 1)
                    s = jnp.where(col <= row, s, NEG_BIG)
                m_prev = m_sc[h]
                m_new = jnp.maximum(m_prev, jnp.max(s, axis=-1, keepdims=True))
                alpha = jnp.exp(m_prev - m_new)
                e = jnp.exp(s - m_new)
                l_sc[h] = alpha * l_sc[h] + jnp.sum(e, axis=-1, keepdims=True)
                m_sc[h] = m_new
                alphas.append(alpha)
                pvs.append(jnp.dot(e.astype(BF16), v, preferred_element_type=F32))
            acc = acc_sc[:, p * LANES:(p + 1) * LANES]
            low = lax.broadcasted_iota(jnp.int32, acc.shape, 1) < (LANES // 2)
            acc_sc[:, p * LANES:(p + 1) * LANES] = jnp.where(low, acc * alphas[0] + pvs[0], acc * alphas[1] + pvs[1])

    @pl.when(ki == 0)
    def _():
        m_sc[...] = jnp.full(m_sc.shape, NEG_BIG, F32)
        l_sc[...] = jnp.zeros(l_sc.shape, F32)
        acc_sc[...] = jnp.zeros(acc_sc.shape, F32)
        if has_prefix:
            update(lambda h: kp_ref[:, h * LANES:(h + 1) * LANES], lambda p: vp_ref[:, p * LANES:(p + 1) * LANES], False)

    main_k = lambda h: k_ref[0, :, h * LANES:(h + 1) * LANES]
    main_v = lambda p: v_ref[0, :, p * LANES:(p + 1) * LANES]

    @pl.when(ki < qi)
    def _():
        update(main_k, main_v, False)

    @pl.when(ki == qi)
    def _():
        update(main_k, main_v, True)
        for p in range(nh // 2):
            low = lax.broadcasted_iota(jnp.int32, (tq, LANES), 1) < (LANES // 2)
            l = jnp.where(low, l_sc[2 * p], l_sc[2 * p + 1])
            o_ref[0, :, p * LANES:(p + 1) * LANES] = acc_sc[:, p * LANES:(p + 1) * LANES] / l


def _flash(q, k, v, prefix, *, nh):
    b, t, _ = q.shape
    dv = v.shape[-1]
    tq = _row_tile(t, 512)
    nt = t // tq
    qspec = pl.BlockSpec((1, tq, nh * LANES), lambda bi, qi, ki: (bi, qi, 0))
    kspec = pl.BlockSpec((1, tq, nh * LANES), lambda bi, qi, ki: (bi, jnp.minimum(ki, qi), 0))
    vspec = pl.BlockSpec((1, tq, dv), lambda bi, qi, ki: (bi, jnp.minimum(ki, qi), 0))
    ins, specs = [q, k, v], [qspec, kspec, vspec]
    if prefix is not None:
        for a in prefix:
            ins.append(a)
            specs.append(pl.BlockSpec(a.shape, lambda bi, qi, ki: (0, 0)))
    return pl.pallas_call(
        functools.partial(_flash_kernel, nh=nh, has_prefix=prefix is not None),
        grid=(b, nt, nt),
        in_specs=specs,
        out_specs=pl.BlockSpec((1, tq, dv), lambda bi, qi, ki: (bi, qi, 0)),
        out_shape=jax.ShapeDtypeStruct((b, t, dv), F32),
        scratch_shapes=[pltpu.VMEM((nh, tq, 1), F32), pltpu.VMEM((nh, tq, 1), F32), pltpu.VMEM((tq, dv), F32)],
        compiler_params=_params("parallel", "parallel", "arbitrary"),
        name="flash_prefix" if prefix is not None else "flash",
    )(*ins)


def _matmul_kernel(a_ref, w_ref, o_ref):
    o_ref[...] = jnp.dot(a_ref[...].astype(BF16), w_ref[...], preferred_element_type=F32).astype(o_ref.dtype)


def _matmul(a, w, out_dtype, name):
    n, k = a.shape
    tm = _row_tile(n, 512)
    return pl.pallas_call(
        _matmul_kernel,
        grid=(n // tm,),
        in_specs=[pl.BlockSpec((tm, k), lambda i: (i, 0)), pl.BlockSpec(w.shape, lambda i: (0, 0))],
        out_specs=pl.BlockSpec((tm, w.shape[1]), lambda i: (i, 0)),
        out_shape=jax.ShapeDtypeStruct((n, w.shape[1]), out_dtype),
        compiler_params=_params("parallel"),
        name=name,
    )(a, w)


def _paged_kernel(pt_ref, q_ref, wabs_ref, wukt_ref, sel_ref, cn_ref, krn_ref, *rest, npg, nh, nope, sd):
    ckv_refs = rest[:npg]
    kr_refs = rest[npg:2 * npg]
    o_ref, lhs_sc, qr_sc, m_sc, l_sc, acc_sc = rest[2 * npg:]
    j = pl.program_id(1)
    nk = nh * nope
    rows = sd * nh

    @pl.when(j == 0)
    def _():
        lhs_sc[0:nk, :] = wukt_ref[...]
        lhs_sc[nk:nk + rows, :] = wabs_ref[0]
        qr_sc[...] = jnp.dot(q_ref[0], sel_ref[...], preferred_element_type=F32).astype(BF16)
        m_sc[...] = jnp.full(m_sc.shape, NEG_BIG, F32)
        l_sc[...] = jnp.zeros(l_sc.shape, F32)
        acc_sc[...] = jnp.zeros(acc_sc.shape, F32)

    def update(c, kr, new_tokens):
        big = _dot_nt(lhs_sc[...], c)
        kraw = big[0:nk]
        ssq = jnp.sum((kraw * kraw).reshape(nh, nope, kraw.shape[-1]), axis=1)
        r = lax.rsqrt(ssq * (1.0 / nope) + EPS)
        s = big[nk:nk + rows] * jnp.concatenate([r] * sd, axis=0) + _dot_nt(qr_sc[...], kr)
        if new_tokens:
            row = lax.broadcasted_iota(jnp.int32, s.shape, 0)
            col = lax.broadcasted_iota(jnp.int32, s.shape, 1)
            s = jnp.where(col * nh <= row, s, NEG_BIG)
        m_prev = m_sc[...]
        m_new = jnp.maximum(m_prev, jnp.max(s, axis=-1, keepdims=True))
        alpha = jnp.exp(m_prev - m_new)
        e = jnp.exp(s - m_new)
        l_sc[...] = alpha * l_sc[...] + jnp.sum(e, axis=-1, keepdims=True)
        m_sc[...] = m_new
        acc_sc[...] = acc_sc[...] * alpha + jnp.dot(e.astype(BF16), c, preferred_element_type=F32)

    c = jnp.concatenate([r[0, 0] for r in ckv_refs], axis=0).astype(BF16)
    kr = jnp.concatenate([r[0, 0] for r in kr_refs], axis=0).astype(BF16)
    update(c, kr, False)

    @pl.when(j == pl.num_programs(1) - 1)
    def _():
        update(cn_ref[0].astype(BF16), krn_ref[0].astype(BF16), True)
        o_ref[0] = acc_sc[...] / l_sc[...]


def _paged_attention(page_table, q, wabs, wukt, sel, cn, krn, cache_ckv, cache_krope, layer, *, nh, nope, sd):
    bd, rows, _ = q.shape
    n_pages = page_table.shape[1]
    npg = PAGES_PER_STEP if n_pages % PAGES_PER_STEP == 0 else 1
    page, kvl = cache_ckv.shape[2:]
    rp = cache_krope.shape[3]
    per_b = lambda a: pl.BlockSpec((1,) + a.shape[1:], lambda b, j, pt: (b, 0, 0))
    const = lambda a: pl.BlockSpec(a.shape, lambda b, j, pt: (0, 0))

    def page_spec(width, p):
        return pl.BlockSpec((1, 1, page, width), lambda b, j, pt: (layer, pt[b, j * npg + p], 0, 0))

    grid_spec = pltpu.PrefetchScalarGridSpec(
        num_scalar_prefetch=1,
        grid=(bd, n_pages // npg),
        in_specs=[per_b(q), per_b(wabs), const(wukt), const(sel), per_b(cn), per_b(krn)]
                 + [page_spec(kvl, p) for p in range(npg)] + [page_spec(rp, p) for p in range(npg)],
        out_specs=pl.BlockSpec((1, rows, kvl), lambda b, j, pt: (b, 0, 0)),
        scratch_shapes=[pltpu.VMEM((nh * nope + rows, kvl), BF16), pltpu.VMEM((rows, rp), BF16),
                        pltpu.VMEM((rows, 1), F32), pltpu.VMEM((rows, 1), F32), pltpu.VMEM((rows, kvl), F32)],
    )
    return pl.pallas_call(
        functools.partial(_paged_kernel, npg=npg, nh=nh, nope=nope, sd=sd),
        grid_spec=grid_spec,
        out_shape=jax.ShapeDtypeStruct((bd, rows, kvl), F32),
        compiler_params=_params("parallel", "arbitrary"),
        name="paged_attention",
    )(page_table, q, wabs, wukt, sel, cn, krn, *([cache_ckv] * npg), *([cache_krope] * npg))


def _block_mask(n, blk):
    r = lax.broadcasted_iota(jnp.int32, (n, n), 0) // blk
    c = lax.broadcasted_iota(jnp.int32, (n, n), 1) // blk
    return r == c


def _split3(x):
    hi = x.astype(BF16)
    r1 = x - hi.astype(F32)
    mid = r1.astype(BF16)
    lo = (r1 - mid.astype(F32)).astype(BF16)
    return hi, mid, lo


def _hgrn_kernel(x_ref, s0_ref, o_ref, sT_ref, st_sc, kp_sc, bp_sc, vp_sc, *, dh, dk, chunk):
    ci = pl.program_id(1)

    @pl.when(ci == 0)
    def _():
        st_sc[...] = s0_ref[0]
        kp_sc[0:chunk, :] = jnp.zeros((chunk, dh), F32)
        bp_sc[0:chunk, :] = jnp.zeros((chunk, dh), F32)
        vp_sc[0:chunk, :] = jnp.zeros((chunk, dh), F32)

    q = x_ref[0, :, 0:dh]
    k = x_ref[0, :, dh:2 * dh]
    g = x_ref[0, :, 2 * dh:3 * dh]
    v = x_ref[0, :, 3 * dh:4 * dh]
    r = lax.broadcasted_iota(jnp.int32, (chunk, chunk), 0)
    c = lax.broadcasted_iota(jnp.int32, (chunk, chunk), 1)
    tri = jnp.where(c <= r, 1.0, 0.0).astype(BF16)
    b = sum(jnp.dot(tri, part, preferred_element_type=F32) for part in _split3(g))
    kp_sc[chunk:2 * chunk, :] = k
    bp_sc[chunk:2 * chunk, :] = b
    vp_sc[chunk:2 * chunk, :] = v
    ones_bd = jnp.where(_block_mask(dh, dk), 1.0, 0.0).astype(BF16)

    o = _dot_nt(q * jnp.exp(b), st_sc[...])
    for d in range(chunk):
        lo = chunk - d
        e = q * kp_sc[lo:lo + chunk, :] * jnp.exp(b - bp_sc[lo:lo + chunk, :])
        o = o + jnp.dot(e.astype(BF16), ones_bd, preferred_element_type=F32) * vp_sc[lo:lo + chunk, :]
    o_ref[0] = o

    b_last = b[chunk - 1:chunk, :]
    upd = _dot_tn(v, k * jnp.exp(b_last - b))
    st_sc[...] = st_sc[...] * jnp.exp(b_last) + jnp.where(_block_mask(dh, dk), upd, 0.0)

    @pl.when(ci == pl.num_programs(1) - 1)
    def _():
        sT_ref[0] = st_sc[...]


def _ret_kernel(x_ref, s0_ref, o_ref, sT_ref, st_sc, *, dh, dk, chunk, n_valid, log_gamma):
    ci = pl.program_id(1)

    @pl.when(ci == 0)
    def _():
        st_sc[...] = s0_ref[0]

    q = x_ref[0, :, 0:dh]
    k = x_ref[0, :, dh:2 * dh]
    v = x_ref[0, :, 2 * dh:3 * dh]
    lane = lax.broadcasted_iota(jnp.int32, (chunk, dh), 1)
    head = lane // dk
    lg = jnp.zeros((chunk, dh), F32)
    for h, val in enumerate(log_gamma):
        lg = jnp.where(head == h, val, lg)
    t = lax.broadcasted_iota(jnp.int32, (chunk, dh), 0).astype(F32)
    r = lax.broadcasted_iota(jnp.int32, (chunk, chunk), 0)
    c = lax.broadcasted_iota(jnp.int32, (chunk, chunk), 1)
    diff = (r - c).astype(F32)

    o = _dot_nt(q, st_sc[...]) * jnp.exp((t + 1.0) * lg)
    for h, val in enumerate(log_gamma):
        in_h = head == h
        a = _dot_nt(jnp.where(in_h, q, 0.0), k)
        a = a * jnp.where(diff >= 0, jnp.exp(jnp.maximum(diff, 0.0) * val), 0.0)
        o = o + jnp.where(in_h, _dot(a, v), 0.0)
    o_ref[0] = o

    upd = _dot_tn(v, k * jnp.exp((n_valid - 1.0 - t) * lg))
    st_sc[...] = st_sc[...] * jnp.exp(n_valid * lg[0:1, :]) + jnp.where(_block_mask(dh, dk), upd, 0.0)

    @pl.when(ci == pl.num_programs(1) - 1)
    def _():
        sT_ref[0] = st_sc[...]


def _recurrence(kind, x, s0, *, dh, dk, chunk, shared_state):
    b, t, w = x.shape
    n_valid = min(chunk, t)
    if t < chunk:
        x = jnp.pad(x, ((0, 0), (0, chunk - t), (0, 0)))
    t_pad = x.shape[1]
    s0_map = (lambda bi, ci: (0, 0, 0)) if shared_state else (lambda bi, ci: (bi, 0, 0))
    if kind == "hgrn":
        body = functools.partial(_hgrn_kernel, dh=dh, dk=dk, chunk=chunk)
        scratch = [pltpu.VMEM((dh, dh), F32)] + [pltpu.VMEM((2 * chunk, dh), F32)] * 3
    else:
        log_gamma = tuple(float(np.log1p(-np.exp2(-5.0 - h))) for h in range(dh // dk))
        body = functools.partial(_ret_kernel, dh=dh, dk=dk, chunk=chunk, n_valid=n_valid, log_gamma=log_gamma)
        scratch = [pltpu.VMEM((dh, dh), F32)]
    return pl.pallas_call(
        body,
        grid=(b, t_pad // chunk),
        in_specs=[pl.BlockSpec((1, chunk, w), lambda bi, ci: (bi, ci, 0)), pl.BlockSpec((1, dh, dh), s0_map)],
        out_specs=[pl.BlockSpec((1, chunk, dh), lambda bi, ci: (bi, ci, 0)),
                   pl.BlockSpec((1, dh, dh), lambda bi, ci: (bi, 0, 0))],
        out_shape=[jax.ShapeDtypeStruct((b, t_pad, dh), F32), jax.ShapeDtypeStruct((b, dh, dh), F32)],
        scratch_shapes=scratch,
        compiler_params=_params("parallel", "arbitrary"),
        name=kind,
    )(x, s0)


def _merge_kernel(om_ref, oh_ref, hg_ref, or_ref, rg_ref, h_ref, w_ref, gh_ref, gr_ref, o_ref, *, nm, dh, dv):
    def gated(o_r, gate_r, g_r):
        o = o_r[...]
        gate = gate_r[...]
        out = []
        for s in range(dh // LANES):
            x = o[:, s * LANES:(s + 1) * LANES]
            lane = lax.broadcasted_iota(jnp.int32, x.shape, 1)
            low = lane < dv
            x2 = x * x
            ss_l = jnp.sum(jnp.where(low, x2, 0.0), axis=-1, keepdims=True)
            ss_h = jnp.sum(jnp.where(low, 0.0, x2), axis=-1, keepdims=True)
            inv = lax.rsqrt(jnp.where(low, ss_l, ss_h) * (1.0 / dv) + EPS)
            gt = gate[:, s * LANES:(s + 1) * LANES]
            out.append(x * inv * g_r[...] * (gt * _sigmoid(gt)))
        return jnp.concatenate(out, axis=-1)

    y = _dot(om_ref[...], w_ref[0:nm, :])
    y = y + _dot(gated(oh_ref, hg_ref, gh_ref), w_ref[nm:nm + dh, :])
    y = y + _dot(gated(or_ref, rg_ref, gr_ref), w_ref[nm + dh:nm + 2 * dh, :])
    o_ref[...] = h_ref[...] + y


def _merge(o_mla, o_hg, hg_pack, o_rt, rt_pack, h, w_o, gh, gr, *, dh, dv):
    n, d = h.shape
    nm = o_mla.shape[1]
    tm = _row_tile(n, 512)
    row = lambda width, cb=0: pl.BlockSpec((tm, width), lambda i: (i, cb))
    const = lambda a: pl.BlockSpec(a.shape, lambda i: (0, 0))
    return pl.pallas_call(
        functools.partial(_merge_kernel, nm=nm, dh=dh, dv=dv),
        grid=(n // tm,),
        in_specs=[row(nm), row(dh), row(dh, 4), row(dh), row(dh, 3), row(d), const(w_o), const(gh), const(gr)],
        out_specs=row(d),
        out_shape=jax.ShapeDtypeStruct((n, d), F32),
        compiler_params=_params("parallel"),
        name="merge",
    )(o_mla, o_hg, hg_pack, o_rt, rt_pack, h, w_o, gh, gr)


def _ffn_kernel(h_ref, g_ref, wg_ref, wu_ref, wd_ref, o_ref, xn_sc, acc_sc):
    j = pl.program_id(1)

    @pl.when(j == 0)
    def _():
        x = h_ref[...]
        xn_sc[...] = (_rms(x, x.shape[-1]) * g_ref[...]).astype(BF16)
        acc_sc[...] = jnp.zeros(acc_sc.shape, F32)

    xn = xn_sc[...]
    a = jnp.dot(xn, wg_ref[...], preferred_element_type=F32)
    u = jnp.dot(xn, wu_ref[...], preferred_element_type=F32)
    acc_sc[...] += jnp.dot((a * _sigmoid(a) * u).astype(BF16), wd_ref[...], preferred_element_type=F32)

    @pl.when(j == pl.num_programs(1) - 1)
    def _():
        o_ref[...] = h_ref[...] + acc_sc[...]


def _ffn(h, g, wg, wu, wd):
    n, d = h.shape
    ff = wg.shape[1]
    tm = _row_tile(n, 512)
    tf = ff // 2 if (ff // 2) % LANES == 0 else ff
    return pl.pallas_call(
        _ffn_kernel,
        grid=(n // tm, ff // tf),
        in_specs=[pl.BlockSpec((tm, d), lambda i, j: (i, 0)), pl.BlockSpec(g.shape, lambda i, j: (0, 0)),
                  pl.BlockSpec((d, tf), lambda i, j: (0, j)), pl.BlockSpec((d, tf), lambda i, j: (0, j)),
                  pl.BlockSpec((tf, d), lambda i, j: (j, 0))],
        out_specs=pl.BlockSpec((tm, d), lambda i, j: (i, 0)),
        out_shape=jax.ShapeDtypeStruct((n, d), F32),
        scratch_shapes=[pltpu.VMEM((tm, d), BF16), pltpu.VMEM((tm, d), F32)],
        compiler_params=_params("parallel", "arbitrary"),
        name="ffn",
    )(h, g, wg, wu, wd)


def _moe_kernel(h_ref, g_ref, wr_ref, wg_ref, wu_ref, wd_ref, o_ref, xn_sc, gate_sc, acc_sc, *, ne):
    e = pl.program_id(1)

    @pl.when(e == 0)
    def _():
        x = h_ref[...]
        xn = _rms(x, x.shape[-1]) * g_ref[...]
        xn_sc[...] = xn.astype(BF16)
        acc_sc[...] = jnp.zeros(acc_sc.shape, F32)
        xh, xm, _ = _split3(xn)
        wh, wm, _ = _split3(wr_ref[...])
        dd = lambda a, b: jnp.dot(a, b, preferred_element_type=F32)
        logits = dd(xh, wh) + (dd(xh, wm) + dd(xm, wh))
        lane = lax.broadcasted_iota(jnp.int32, logits.shape, 1)
        valid = lane < ne
        lg = jnp.where(valid, logits, -jnp.inf)
        v1 = jnp.max(lg, axis=-1, keepdims=True)
        i1 = jnp.min(jnp.where(lg == v1, lane, LANES), axis=-1, keepdims=True)
        lg2 = jnp.where(lane == i1, -jnp.inf, lg)
        v2 = jnp.max(lg2, axis=-1, keepdims=True)
        i2 = jnp.min(jnp.where(lg2 == v2, lane, LANES), axis=-1, keepdims=True)
        e2 = jnp.exp(v2 - v1)
        w1 = 1.0 / (1.0 + e2)
        w2 = e2 / (1.0 + e2)
        gate_sc[...] = jnp.where(lane == i1, w1, jnp.where(lane == i2, w2, 0.0))

    lane = lax.broadcasted_iota(jnp.int32, gate_sc.shape, 1)
    gate = jnp.sum(jnp.where(lane == e, gate_sc[...], 0.0), axis=-1, keepdims=True)
    xn = xn_sc[...]
    a = jnp.dot(xn, wg_ref[0], preferred_element_type=F32)
    u = jnp.dot(xn, wu_ref[0], preferred_element_type=F32)
    acc_sc[...] += gate * jnp.dot((a * _sigmoid(a) * u).astype(BF16), wd_ref[0], preferred_element_type=F32)

    @pl.when(e == ne - 1)
    def _():
        o_ref[...] = h_ref[...] + acc_sc[...]


def _moe(h, g, wr, wg, wu, wd):
    n, d = h.shape
    ne, _, ff = wg.shape
    tm = _row_tile(n, 512)
    return pl.pallas_call(
        functools.partial(_moe_kernel, ne=ne),
        grid=(n // tm, ne),
        in_specs=[pl.BlockSpec((tm, d), lambda i, e: (i, 0)), pl.BlockSpec(g.shape, lambda i, e: (0, 0)),
                  pl.BlockSpec(wr.shape, lambda i, e: (0, 0)),
                  pl.BlockSpec((1, d, ff), lambda i, e: (e, 0, 0)), pl.BlockSpec((1, d, ff), lambda i, e: (e, 0, 0)),
                  pl.BlockSpec((1, ff, d), lambda i, e: (e, 0, 0))],
        out_specs=pl.BlockSpec((tm, d), lambda i, e: (i, 0)),
        out_shape=jax.ShapeDtypeStruct((n, d), F32),
        scratch_shapes=[pltpu.VMEM((tm, d), BF16), pltpu.VMEM((tm, LANES), F32), pltpu.VMEM((tm, d), F32)],
        compiler_params=_params("parallel", "arbitrary"),
        name="moe",
    )(h, g, wr, wg, wu, wd)


def _rope_tables(pos, rope_m, ret_dk):
    def angles(d):
        inv = ROPE_BASE ** (-jnp.arange(0, d, 2, dtype=F32) / d)
        return pos.astype(F32)[:, None] * inv[None, :]

    n = pos.shape[0]
    a = angles(rope_m)
    c, s = jnp.cos(a), jnp.sin(a)
    pad = LANES - 64 - rope_m
    cm = jnp.concatenate([jnp.ones((n, 64), F32), c, c, jnp.ones((n, pad), F32)], axis=1)
    sm = jnp.concatenate([jnp.zeros((n, 64), F32), -s, s, jnp.zeros((n, pad), F32)], axis=1)
    a = angles(ret_dk)
    c, s = jnp.cos(a), jnp.sin(a)
    reps = LANES // ret_dk
    cr = jnp.concatenate([c, c] * reps, axis=1)
    sr = jnp.concatenate([-s, s] * reps, axis=1)
    return cm, sm, cr, sr


def _slab_cols(w, nope, rope_m):
    r, nh, dd = w.shape
    return jnp.pad(w, ((0, 0), (0, 0), (0, LANES - dd))).reshape(r, nh * LANES)


def _block_diag_t(s):
    b, nh, dk, dv = s.shape
    eye = jnp.eye(nh, dtype=s.dtype)
    return jnp.einsum('bhkv,hg->bhvgk', s, eye).reshape(b, nh * dv, nh * dk)


def _unblock_t(st, nh):
    b, n, _ = st.shape
    dd = n // nh
    blocks = st.reshape(b, nh, dd, nh, dd)
    return jnp.stack([blocks[:, h, :, h, :] for h in range(nh)], axis=1).swapaxes(-1, -2)


def kernel(x_prompt, x_sample, cache_ckv, cache_krope, page_table, state_hgrn, state_ret, meta_tokens,
           norm_mix_g, w_in, q_lora_g, kv_lora_g, w_uq, w_uk, w_uv, qn_nope_g, qn_rope_g, kn_nope_g, kn_rope_g,
           hg_lb, hg_norm_g, ret_norm_g, w_o, norm_ffn_g, ffn_w_gate, ffn_w_up, ffn_w_down,
           moe_router, moe_w_gate, moe_w_up, moe_w_down):
    depth = w_in.shape[0]
    B, S, D = x_prompt.shape
    Bd, Sd, _ = x_sample.shape
    n_meta = meta_tokens.shape[0]
    nq, nkv = q_lora_g.shape[1], kv_lora_g.shape[1]
    nh, nope, rope_m = w_uq.shape[2], qn_nope_g.shape[1], qn_rope_g.shape[1]
    vdim = w_uv.shape[3]
    hg_heads, hg_dk = state_hgrn.shape[2], state_hgrn.shape[3]
    rt_heads, rt_dk = state_ret.shape[2], state_ret.shape[3]
    dh = hg_heads * hg_dk
    assert nope == 64 and rope_m == 32 and vdim == 64 and hg_dk == 64 and rt_dk == 64
    assert dh == rt_heads * rt_dk and nh % 2 == 0
    past_len = page_table.shape[1] * cache_ckv.shape[2]
    scale = float((nope + rope_m) ** -0.5)
    ns = Bd * Sd

    h_b = x_prompt.reshape(B * S, D)
    h_s = jnp.concatenate([x_sample.reshape(ns, D), meta_tokens.astype(x_prompt.dtype)], axis=0)
    pos_b = jnp.tile(jnp.arange(S, dtype=F32) + n_meta, B)
    pos_s = jnp.concatenate([jnp.tile(jnp.arange(Sd, dtype=F32) + past_len, Bd), jnp.arange(n_meta, dtype=F32)])
    tabs_b = _rope_tables(pos_b, rope_m, rt_dk)
    tabs_s = _rope_tables(pos_s, rope_m, rt_dk)

    lb_soft = jax.nn.softmax(hg_lb.astype(F32), axis=0)
    lb_all = jnp.cumsum(lb_soft, axis=0) - lb_soft[0]

    lane_pad = lambda g, lo: jnp.pad(g, (lo, LANES - lo - g.shape[0]))[None, :]
    sel = jnp.zeros((LANES, rope_m), BF16).at[jnp.arange(rope_m) + 64, jnp.arange(rope_m)].set(1.0)

    outs = {k: [] for k in ("ckv_p", "kr_p", "hg_p", "rt_p", "ckv_s", "kr_s", "hg_s", "rt_s")}
    for l in range(depth):
        wi = w_in[l]
        o_kr = nq + nkv
        w_perm = jnp.concatenate(
            [wi[:, :o_kr], jnp.zeros((D, 64), F32), wi[:, o_kr:o_kr + rope_m], jnp.zeros((D, LANES - 64 - rope_m), F32),
             wi[:, o_kr + rope_m:]], axis=1).astype(BF16)
        wq = _slab_cols(w_uq[l], nope, rope_m).astype(BF16)
        wk = _slab_cols(w_uk[l], nope, rope_m).astype(BF16)
        wv = w_uv[l].reshape(nkv, nh * vdim).astype(BF16)
        wukt = w_uk[l].reshape(nkv, nh * nope).T.astype(BF16)
        wabs_w = jnp.einsum('rhd,hg->hdgr', w_uk[l], jnp.eye(nh, dtype=F32))
        wabs_w = jnp.pad(wabs_w, ((0, 0), (0, LANES - nope), (0, 0), (0, 0))).reshape(nh * LANES, nh * nkv).astype(BF16)
        wuv_bd = jnp.einsum('rhd,hg->hrgd', w_uv[l], jnp.eye(nh, dtype=F32)).reshape(nh * nkv, nh * vdim).astype(BF16)
        g_q = (lane_pad(qn_nope_g[l], 0) + lane_pad(qn_rope_g[l], 64)).astype(F32)
        g_k = lane_pad(kn_nope_g[l], 0).astype(F32)
        g_kr = lane_pad(kn_rope_g[l], 64).astype(F32)
        g_hg = jnp.tile(hg_norm_g[l], LANES // hg_dk)[None, :]
        g_rt = jnp.tile(ret_norm_g[l], LANES // rt_dk)[None, :]
        wo = w_o[l].astype(BF16)
        inproj = functools.partial(_inproj, gn=norm_mix_g[l][None, :], w=w_perm, gq=q_lora_g[l][None, :],
                                   gkv=kv_lora_g[l][None, :], gkr=g_kr, lb=lb_all[l][None, :],
                                   nq=nq, nkv=nkv, rope_m=rope_m, dh=dh)
        qprep = functools.partial(_qprep, w=wq, g=g_q, gk=g_k, nh=nh, nope=nope, rope_m=rope_m, scale=scale)

        cq_b, ckv_b, krs_b, hg_b, rt_b = inproj(h_b, tabs=tabs_b)
        cq_s, ckv_s, krs_s, hg_s, rt_s = inproj(h_s, tabs=tabs_s)
        (q_b,) = qprep(cq_b, cm=tabs_b[0], sm=tabs_b[1], with_qg=False)
        q_s, qg_s = qprep(cq_s, cm=tabs_s[0], sm=tabs_s[1], with_qg=True)
        k_b, v_b = _kvup(ckv_b, krs_b, wk, wv, g_k, nh=nh, nope=nope)
        k_m, v_m = _kvup(ckv_s[ns:], krs_s[ns:], wk, wv, g_k, nh=nh, nope=nope)

        o_b = _flash(q_b.reshape(B, S, -1), k_b.reshape(B, S, -1), v_b.reshape(B, S, -1), (k_m, v_m), nh=nh)
        o_m = _flash(q_s[ns:][None], k_m[None], v_m[None], None, nh=nh)
        wabs = _matmul(qg_s[:ns], wabs_w, BF16, "qabsorb").reshape(Bd, Sd * nh, nkv)
        pad_new = lambda a: jnp.pad(a.reshape(Bd, Sd, -1), ((0, 0), (0, LANES - Sd), (0, 0)))
        kr_s_compact = krs_s[:, 64:64 + rope_m]
        acc = _paged_attention(page_table, q_s[:ns].reshape(Bd, Sd * nh, LANES), wabs, wukt, sel,
                               pad_new(ckv_s[:ns]), pad_new(kr_s_compact[:ns]), cache_ckv, cache_krope, l,
                               nh=nh, nope=nope, sd=Sd)
        o_s = _matmul(acc.reshape(ns, nh * nkv), wuv_bd, F32, "vabsorb")
        o_small = jnp.concatenate([o_s, o_m[0]], axis=0)

        zero_state = jnp.zeros((1, dh, dh), F32)
        rec = functools.partial(_recurrence, dh=dh, dk=hg_dk)
        ohg_m, sT_hg_m = rec("hgrn", hg_s[ns:][None], zero_state, chunk=SMALL_CHUNK, shared_state=True)
        ohg_s, sT_hg_s = rec("hgrn", hg_s[:ns].reshape(Bd, Sd, -1), _block_diag_t(state_hgrn[l].astype(F32)),
                             chunk=SMALL_CHUNK, shared_state=False)
        ohg_b, sT_hg_b = rec("hgrn", hg_b.reshape(B, S, -1), sT_hg_m, chunk=HG_CHUNK, shared_state=True)
        ort_m, sT_rt_m = rec("ret", rt_s[ns:][None], zero_state, chunk=SMALL_CHUNK, shared_state=True)
        ort_s, sT_rt_s = rec("ret", rt_s[:ns].reshape(Bd, Sd, -1), _block_diag_t(state_ret[l].astype(F32)),
                             chunk=SMALL_CHUNK, shared_state=False)
        ort_b, sT_rt_b = rec("ret", rt_b.reshape(B, S, -1), sT_rt_m, chunk=RET_CHUNK, shared_state=True)
        ohg_small = jnp.concatenate([ohg_s[:, :Sd].reshape(ns, dh), ohg_m[0, :n_meta]], axis=0)
        ort_small = jnp.concatenate([ort_s[:, :Sd].reshape(ns, dh), ort_m[0, :n_meta]], axis=0)

        merge = functools.partial(_merge, w_o=wo, gh=g_hg, gr=g_rt, dh=dh, dv=hg_dk)
        h_b = merge(o_b.reshape(B * S, -1), ohg_b.reshape(B * S, dh), hg_b, ort_b.reshape(B * S, dh), rt_b, h_b)
        h_s = merge(o_small, ohg_small, hg_s, ort_small, rt_s, h_s)
        g_ffn = norm_ffn_g[l][None, :]
        i = l // 2
        if l % 2 == 0:
            mix = functools.partial(_ffn, g=g_ffn, wg=ffn_w_gate[i].astype(BF16), wu=ffn_w_up[i].astype(BF16),
                                    wd=ffn_w_down[i].astype(BF16))
        else:
            ne = moe_router.shape[2]
            wr = jnp.pad(moe_router[i], ((0, 0), (0, LANES - ne)))
            mix = functools.partial(_moe, g=g_ffn, wr=wr, wg=moe_w_gate[i].astype(BF16), wu=moe_w_up[i].astype(BF16),
                                    wd=moe_w_down[i].astype(BF16))
        h_b = mix(h_b)
        h_s = mix(h_s)

        with_meta = lambda real, meta: jnp.concatenate(
            [jnp.broadcast_to(meta[None], (B,) + meta.shape), real.reshape(B, S, -1)], axis=1)
        outs["ckv_p"].append(with_meta(ckv_b, ckv_s[ns:]))
        outs["kr_p"].append(with_meta(krs_b[:, 64:64 + rope_m], kr_s_compact[ns:]))
        outs["hg_p"].append(_unblock_t(sT_hg_b, hg_heads).astype(state_hgrn.dtype))
        outs["rt_p"].append(_unblock_t(sT_rt_b, rt_heads).astype(state_ret.dtype))
        outs["ckv_s"].append(ckv_s[:ns].reshape(Bd, Sd, nkv))
        outs["kr_s"].append(kr_s_compact[:ns].reshape(Bd, Sd, rope_m))
        outs["hg_s"].append(_unblock_t(sT_hg_s, hg_heads).astype(state_hgrn.dtype))
        outs["rt_s"].append(_unblock_t(sT_rt_s, rt_heads).astype(state_ret.dtype))

    y_prompt = h_b.reshape(B, S, D)
    y_sample = h_s[:ns].reshape(Bd, Sd, D)
    stack = lambda k: jnp.stack(outs[k])
    return (y_prompt, y_sample, stack("ckv_p"), stack("kr_p"), stack("hg_p"), stack("rt_p"),
            stack("ckv_s"), stack("kr_s"), stack("hg_s"), stack("rt_s"))
```

```python
import functools

import numpy as np
import jax
import jax.numpy as jnp
from jax import lax
from jax.experimental import pallas as pl
from jax.experimental.pallas import tpu as pltpu

F32 = jnp.float32
BF16 = jnp.bfloat16

EPS = 1e-6
ROPE_BASE = 10000.0
NEG_BIG = -1e30
LOG2E = 1.4426950408889634
LANES = 128
SUBLANES = 8
VMEM_LIMIT = 56 * 1024 * 1024

HG_CHUNK = 64
RET_CHUNK = 128
SMALL_CHUNK = 16
PAGES_PER_STEP = 16
QUERY_SPLIT = 1
FLASH_TILE = 1024


def _params(*sem):
    return pltpu.CompilerParams(dimension_semantics=sem, vmem_limit_bytes=VMEM_LIMIT)


def _row_tile(n, pref):
    if n <= pref:
        return n
    t = pref - pref % 16
    while t >= 16:
        if n % t == 0:
            return t
        t -= 16
    return n


def _dot(a, b):
    return jnp.dot(a.astype(BF16), b.astype(BF16), preferred_element_type=F32)


def _dot_nt(a, b):
    return lax.dot_general(a.astype(BF16), b.astype(BF16), (((1,), (1,)), ((), ())), preferred_element_type=F32)


def _dot_tn(a, b):
    return lax.dot_general(a.astype(BF16), b.astype(BF16), (((0,), (0,)), ((), ())), preferred_element_type=F32)


def _rms(x, width):
    return x * lax.rsqrt(jnp.sum(x * x, axis=-1, keepdims=True) * (1.0 / width) + EPS)


def _rope128(y, cos, sin_signed, half):
    lane = lax.broadcasted_iota(jnp.int32, y.shape, 1)
    first = (lane % (2 * half)) < half
    rot = jnp.where(first, pltpu.roll(y, LANES - half, 1), pltpu.roll(y, half, 1))
    return y * cos + rot * sin_signed


def _sigmoid(x):
    return 1.0 / (1.0 + jnp.exp(-x))


def _inproj_kernel(h_ref, gn_ref, w_ref, gq_ref, gkv_ref, gkr_ref, lb_ref, cm_ref, sm_ref, cr_ref, sr_ref,
                   cq_ref, ckv_ref, kr_ref, hg_ref, rt_ref, *, nq, nkv, rope_m, dh):
    x = h_ref[...]
    xb = (_rms(x, x.shape[-1]) * gn_ref[...]).astype(BF16)

    def proj(lo, width):
        return jnp.dot(xb, w_ref[:, lo:lo + width], preferred_element_type=F32)

    c = proj(0, nq)
    cq_ref[...] = (_rms(c, nq) * gq_ref[...]).astype(BF16)
    c = proj(nq, nkv)
    ckv_ref[...] = _rms(c, nkv) * gkv_ref[...]
    k = proj(nq + nkv, LANES)
    k = _rms(k, rope_m) * gkr_ref[...]
    kr_ref[...] = _rope128(k, cm_ref[...], sm_ref[...], rope_m // 2)

    base = nq + nkv + LANES
    hq = proj(base, dh)
    hg_ref[:, 0:dh] = hq * _sigmoid(hq)
    z = proj(base + dh, dh)
    lb = lb_ref[...]
    f = lb + (1.0 - lb) * _sigmoid(z)
    hg_ref[:, dh:2 * dh] = 1.0 - f
    hg_ref[:, 2 * dh:3 * dh] = jnp.log(f)
    hg_ref[:, 3 * dh:4 * dh] = proj(base + 2 * dh, dh)
    hg_ref[:, 4 * dh:5 * dh] = proj(base + 3 * dh, dh)

    base = base + 4 * dh
    cr = cr_ref[...]
    sr = sr_ref[...]
    for part, scale in ((0, 1.0), (1, 64 ** -0.5)):
        for s in range(dh // LANES):
            lo = part * dh + s * LANES
            y = _rope128(proj(base + lo, LANES), cr, sr, 32)
            rt_ref[:, lo:lo + LANES] = y * scale
    rt_ref[:, 2 * dh:3 * dh] = proj(base + 2 * dh, dh)
    rt_ref[:, 3 * dh:4 * dh] = proj(base + 3 * dh, dh)


def _inproj(h, gn, w, gq, gkv, gkr, lb, tabs, *, nq, nkv, rope_m, dh):
    n, d = h.shape
    tm = _row_tile(n, 512)
    cm, sm, cr, sr = tabs
    row = lambda width: pl.BlockSpec((tm, width), lambda i: (i, 0))
    const = lambda a: pl.BlockSpec(a.shape, lambda i: (0, 0))
    return pl.pallas_call(
        functools.partial(_inproj_kernel, nq=nq, nkv=nkv, rope_m=rope_m, dh=dh),
        grid=(n // tm,),
        in_specs=[row(d), const(gn), const(w), const(gq), const(gkv), const(gkr), const(lb),
                  row(LANES), row(LANES), row(LANES), row(LANES)],
        out_specs=[row(nq), row(nkv), row(LANES), row(5 * dh), row(4 * dh)],
        out_shape=[jax.ShapeDtypeStruct((n, nq), BF16), jax.ShapeDtypeStruct((n, nkv), F32),
                   jax.ShapeDtypeStruct((n, LANES), F32), jax.ShapeDtypeStruct((n, 5 * dh), F32),
                   jax.ShapeDtypeStruct((n, 4 * dh), F32)],
        compiler_params=_params("parallel"),
        name="inproj",
    )(h, gn, w, gq, gkv, gkr, lb, cm, sm, cr, sr)


def _qprep_kernel(c_ref, w_ref, g_ref, gk_ref, cm_ref, sm_ref, *out_refs, nh, nope, rope_m, scale, with_qg):
    c = c_ref[...]
    cm = cm_ref[...]
    sm = sm_ref[...]
    g = g_ref[...]
    for h in range(nh):
        x = jnp.dot(c, w_ref[:, h * LANES:(h + 1) * LANES], preferred_element_type=F32)
        lane = lax.broadcasted_iota(jnp.int32, x.shape, 1)
        is_nope = lane < nope
        x2 = x * x
        ss_n = jnp.sum(jnp.where(is_nope, x2, 0.0), axis=-1, keepdims=True)
        ss_r = jnp.sum(jnp.where(is_nope, 0.0, x2), axis=-1, keepdims=True)
        inv = jnp.where(is_nope, lax.rsqrt(ss_n * (1.0 / nope) + EPS), lax.rsqrt(ss_r * (1.0 / rope_m) + EPS))
        y = _rope128(x * inv * g, cm, sm, rope_m // 2) * scale
        out_refs[0][:, h * LANES:(h + 1) * LANES] = y.astype(BF16)
        if with_qg:
            out_refs[1][:, h * LANES:(h + 1) * LANES] = (y * gk_ref[...]).astype(BF16)


def _qprep(cqn, w, g, gk, cm, sm, *, nh, nope, rope_m, scale, with_qg):
    n, r = cqn.shape
    tm = _row_tile(n, 512)
    row = lambda width: pl.BlockSpec((tm, width), lambda i: (i, 0))
    const = lambda a: pl.BlockSpec(a.shape, lambda i: (0, 0))
    nout = 2 if with_qg else 1
    return pl.pallas_call(
        functools.partial(_qprep_kernel, nh=nh, nope=nope, rope_m=rope_m, scale=scale, with_qg=with_qg),
        grid=(n // tm,),
        in_specs=[row(r), const(w), const(g), const(gk), row(LANES), row(LANES)],
        out_specs=[row(nh * LANES)] * nout,
        out_shape=[jax.ShapeDtypeStruct((n, nh * LANES), BF16)] * nout,
        compiler_params=_params("parallel"),
        name="qprep",
    )(cqn, w, g, gk, cm, sm)


def _kvup_kernel(c_ref, kr_ref, wk_ref, wvt_ref, gk_ref, k_ref, vt_ref, *, nh, nope):
    c = c_ref[...].astype(BF16)
    kr = kr_ref[...]
    gk = gk_ref[...]
    for h in range(nh):
        x = jnp.dot(c, wk_ref[:, h * LANES:(h + 1) * LANES], preferred_element_type=F32)
        k_ref[:, h * LANES:(h + 1) * LANES] = (_rms(x, nope) * gk + kr).astype(BF16)
    vt_ref[...] = _dot_nt(wvt_ref[...], c).astype(BF16)


def _kvup(ckv, krs, wk, wvt, gk, *, nh, nope):
    n, r = ckv.shape
    tm = _row_tile(n, 512)
    row = lambda width: pl.BlockSpec((tm, width), lambda i: (i, 0))
    const = lambda a: pl.BlockSpec(a.shape, lambda i: (0, 0))
    nv = wvt.shape[0]
    return pl.pallas_call(
        functools.partial(_kvup_kernel, nh=nh, nope=nope),
        grid=(n // tm,),
        in_specs=[row(r), row(LANES), const(wk), const(wvt), const(gk)],
        out_specs=[row(nh * LANES), pl.BlockSpec((nv, tm), lambda i: (0, i))],
        out_shape=[jax.ShapeDtypeStruct((n, nh * LANES), BF16), jax.ShapeDtypeStruct((nv, n), BF16)],
        compiler_params=_params("parallel"),
        name="kvup",
    )(ckv, krs, wk, wvt, gk)


def _flash_kernel(*refs, nh, vdim, has_prefix):
    if has_prefix:
        q_ref, k_ref, vt_ref, kp_ref, vtp_ref, o_ref, m_sc, l_sc, acc_sc = refs
    else:
        q_ref, k_ref, vt_ref, o_ref, m_sc, l_sc, acc_sc = refs
    qi = pl.program_id(1)
    ki = pl.program_id(2)

    def update(k_r, vt_r, masked):
        tq = q_ref.shape[0]
        ts = tq // QUERY_SPLIT if tq % (QUERY_SPLIT * LANES) == 0 else tq
        for h in range(nh):
            rows = slice(h * vdim, (h + 1) * vdim)
            for c in range(tq // ts):
                qs = slice(c * ts, (c + 1) * ts)
                st = _dot_nt(k_r[:, h * LANES:(h + 1) * LANES], q_ref[qs, h * LANES:(h + 1) * LANES])
                if masked:
                    key = lax.broadcasted_iota(jnp.int32, st.shape, 0)
                    qry = lax.broadcasted_iota(jnp.int32, st.shape, 1) + c * ts
                    st = jnp.where(key <= qry, st, NEG_BIG)
                m_prev = m_sc[h:h + 1, qs]
                m_new = jnp.maximum(m_prev, jnp.max(st, axis=0, keepdims=True))
                alpha = jnp.exp2(m_prev - m_new)
                e = jnp.exp2(st - m_new)
                l_sc[h:h + 1, qs] = alpha * l_sc[h:h + 1, qs] + jnp.sum(e, axis=0, keepdims=True)
                m_sc[h:h + 1, qs] = m_new
                pv = jnp.dot(vt_r[rows, :], e.astype(BF16), preferred_element_type=F32)
                acc_sc[rows, qs] = acc_sc[rows, qs] * alpha + pv

    @pl.when(ki == 0)
    def _():
        m_sc[...] = jnp.full(m_sc.shape, NEG_BIG, F32)
        l_sc[...] = jnp.zeros(l_sc.shape, F32)
        acc_sc[...] = jnp.zeros(acc_sc.shape, F32)
        if has_prefix:
            update(kp_ref, vtp_ref, False)

    @pl.when(ki < qi)
    def _():
        update(k_ref, vt_ref, False)

    @pl.when(ki == qi)
    def _():
        update(k_ref, vt_ref, True)
        for h in range(nh):
            rows = slice(h * vdim, (h + 1) * vdim)
            acc_sc[rows, :] = acc_sc[rows, :] / l_sc[h:h + 1, :]
        o_ref[...] = acc_sc[...].T


def _flash(q, k, vt, prefix, *, nh, batch):
    n = q.shape[0]
    t = n // batch
    dv = vt.shape[0]
    tq = _row_tile(t, FLASH_TILE)
    nt = t // tq
    qspec = pl.BlockSpec((tq, nh * LANES), lambda bi, qi, ki: (bi * nt + qi, 0))
    kspec = pl.BlockSpec((tq, nh * LANES), lambda bi, qi, ki: (bi * nt + jnp.minimum(ki, qi), 0))
    vspec = pl.BlockSpec((dv, tq), lambda bi, qi, ki: (0, bi * nt + jnp.minimum(ki, qi)))
    ins, specs = [q, k, vt], [qspec, kspec, vspec]
    if prefix is not None:
        for a in prefix:
            ins.append(a)
            specs.append(pl.BlockSpec(a.shape, lambda bi, qi, ki: (0, 0)))
    return pl.pallas_call(
        functools.partial(_flash_kernel, nh=nh, vdim=dv // nh, has_prefix=prefix is not None),
        grid=(batch, nt, nt),
        in_specs=specs,
        out_specs=pl.BlockSpec((tq, dv), lambda bi, qi, ki: (bi * nt + qi, 0)),
        out_shape=jax.ShapeDtypeStruct((n, dv), F32),
        scratch_shapes=[pltpu.VMEM((nh, tq), F32), pltpu.VMEM((nh, tq), F32), pltpu.VMEM((dv, tq), F32)],
        compiler_params=_params("parallel", "parallel", "arbitrary"),
        name="flash_prefix" if prefix is not None else "flash",
    )(*ins)


def _matmul_kernel(a_ref, w_ref, o_ref):
    o_ref[...] = jnp.dot(a_ref[...].astype(BF16), w_ref[...], preferred_element_type=F32).astype(o_ref.dtype)


def _matmul(a, w, out_dtype, name):
    n, k = a.shape
    tm = _row_tile(n, 512)
    return pl.pallas_call(
        _matmul_kernel,
        grid=(n // tm,),
        in_specs=[pl.BlockSpec((tm, k), lambda i: (i, 0)), pl.BlockSpec(w.shape, lambda i: (0, 0))],
        out_specs=pl.BlockSpec((tm, w.shape[1]), lambda i: (i, 0)),
        out_shape=jax.ShapeDtypeStruct((n, w.shape[1]), out_dtype),
        compiler_params=_params("parallel"),
        name=name,
    )(a, w)


def _paged_kernel(pt_ref, q_ref, wabs_ref, wukt_ref, sel_ref, cn_ref, krn_ref, *rest, npg, sub, nh, nope, sd):
    ckv_refs = rest[:npg]
    krt_refs = rest[npg:2 * npg]
    o_ref, lhs_sc, qr_sc, m_sc, l_sc, acc_sc = rest[2 * npg:]
    j = pl.program_id(1)
    nk = nh * nope
    rows = sd * nh

    @pl.when(j == 0)
    def _():
        lhs_sc[0:nk, :] = wukt_ref[...]
        lhs_sc[nk:nk + rows, :] = wabs_ref[0]
        qr_sc[...] = jnp.dot(q_ref[0], sel_ref[...], preferred_element_type=F32).astype(BF16)
        m_sc[...] = jnp.full(m_sc.shape, NEG_BIG, F32)
        l_sc[...] = jnp.zeros(l_sc.shape, F32)
        acc_sc[...] = jnp.zeros(acc_sc.shape, F32)

    def scores(c, krt):
        big = _dot_nt(lhs_sc[...], c)
        kraw = big[0:nk]
        ssq = jnp.sum((kraw * kraw).reshape(nh, nope, kraw.shape[-1]), axis=1)
        r = lax.rsqrt(ssq * (1.0 / nope) + EPS)
        return big[nk:nk + rows] * jnp.concatenate([r] * sd, axis=0) + jnp.dot(
            qr_sc[...], krt, preferred_element_type=F32)

    def update(s, c):
        m_prev = m_sc[...]
        m_new = jnp.maximum(m_prev, jnp.max(s, axis=-1, keepdims=True))
        alpha = jnp.exp2(m_prev - m_new)
        e = jnp.exp2(s - m_new)
        l_sc[...] = alpha * l_sc[...] + jnp.sum(e, axis=-1, keepdims=True)
        m_sc[...] = m_new
        acc_sc[...] = acc_sc[...] * alpha + jnp.dot(e.astype(BF16), c, preferred_element_type=F32)

    cs, ss = [], []
    for g in range(0, npg, sub):
        c = jnp.concatenate([r[0, 0] for r in ckv_refs[g:g + sub]], axis=0).astype(BF16)
        krt = jnp.concatenate([r[0, 0] for r in krt_refs[g:g + sub]], axis=1).astype(BF16)
        cs.append(c)
        ss.append(scores(c, krt))
    update(jnp.concatenate(ss, axis=1), jnp.concatenate(cs, axis=0))

    @pl.when(j == pl.num_programs(1) - 1)
    def _():
        c = cn_ref[0].astype(BF16)
        s = scores(c, krn_ref[0].astype(BF16))
        row = lax.broadcasted_iota(jnp.int32, s.shape, 0)
        col = lax.broadcasted_iota(jnp.int32, s.shape, 1)
        update(jnp.where(col * nh <= row, s, NEG_BIG), c)
        o_ref[0] = acc_sc[...] / l_sc[...]


def _paged_attention(page_table, q, wabs, wukt, sel, cn, krnt, cache_ckv, cache_kropet, layer, *, nh, nope, sd):
    bd, rows, _ = q.shape
    n_pages = page_table.shape[1]
    npg = PAGES_PER_STEP if n_pages % PAGES_PER_STEP == 0 else 1
    sub = 2 if npg % 2 == 0 else 1
    page, kvl = cache_ckv.shape[2:]
    rp = cache_kropet.shape[2]
    per_b = lambda a: pl.BlockSpec((1,) + a.shape[1:], lambda b, j, pt: (b, 0, 0))
    const = lambda a: pl.BlockSpec(a.shape, lambda b, j, pt: (0, 0))

    def page_spec(shape, p):
        return pl.BlockSpec((1, 1) + shape, lambda b, j, pt: (layer, pt[b, j * npg + p], 0, 0))

    grid_spec = pltpu.PrefetchScalarGridSpec(
        num_scalar_prefetch=1,
        grid=(bd, n_pages // npg),
        in_specs=[per_b(q), per_b(wabs), const(wukt), const(sel), per_b(cn), per_b(krnt)]
                 + [page_spec((page, kvl), p) for p in range(npg)] + [page_spec((rp, page), p) for p in range(npg)],
        out_specs=pl.BlockSpec((1, rows, kvl), lambda b, j, pt: (b, 0, 0)),
        scratch_shapes=[pltpu.VMEM((nh * nope + rows, kvl), BF16), pltpu.VMEM((rows, rp), BF16),
                        pltpu.VMEM((rows, 1), F32), pltpu.VMEM((rows, 1), F32), pltpu.VMEM((rows, kvl), F32)],
    )
    return pl.pallas_call(
        functools.partial(_paged_kernel, npg=npg, sub=sub, nh=nh, nope=nope, sd=sd),
        grid_spec=grid_spec,
        out_shape=jax.ShapeDtypeStruct((bd, rows, kvl), F32),
        compiler_params=_params("parallel", "arbitrary"),
        name="paged_attention",
    )(page_table, q, wabs, wukt, sel, cn, krnt, *([cache_ckv] * npg), *([cache_kropet] * npg))


def _block_mask(n, blk):
    r = lax.broadcasted_iota(jnp.int32, (n, n), 0) // blk
    c = lax.broadcasted_iota(jnp.int32, (n, n), 1) // blk
    return r == c


def _split3(x):
    hi = x.astype(BF16)
    r1 = x - hi.astype(F32)
    mid = r1.astype(BF16)
    lo = (r1 - mid.astype(F32)).astype(BF16)
    return hi, mid, lo


def _hgrn_kernel(x_ref, s0_ref, o_ref, sT_ref, st_sc, pad_sc, sh_sc, *, dh, dk, chunk):
    ci = pl.program_id(1)

    @pl.when(ci == 0)
    def _():
        st_sc[...] = s0_ref[0]
        pad_sc[:, 0:SUBLANES, :] = jnp.zeros((3, SUBLANES, dh), F32)

    q = x_ref[0, :, 0:dh]
    k = x_ref[0, :, dh:2 * dh]
    g = x_ref[0, :, 2 * dh:3 * dh]
    v = x_ref[0, :, 3 * dh:4 * dh]
    r = lax.broadcasted_iota(jnp.int32, (chunk, chunk), 0)
    c = lax.broadcasted_iota(jnp.int32, (chunk, chunk), 1)
    tri = jnp.where(c <= r, 1.0, 0.0).astype(BF16)
    b = sum(jnp.dot(tri, part, preferred_element_type=F32) for part in _split3(g * LOG2E))
    for i, val in enumerate((k, b, v)):
        pad_sc[i, SUBLANES:SUBLANES + chunk, :] = val
        sh_sc[i, 0] = val
        for r in range(1, SUBLANES):
            sh_sc[i, r] = pad_sc[i, SUBLANES - r:SUBLANES - r + chunk, :]
    ones_bd = jnp.where(_block_mask(dh, dk), 1.0, 0.0).astype(BF16)

    o_ref[0] = _dot_nt(q * jnp.exp2(b), st_sc[...])
    for a in range(chunk // SUBLANES):
        lo = a * SUBLANES
        rows = chunk - lo
        qa = q[lo:, :]
        ba = b[lo:, :]
        acc = None
        for r in range(SUBLANES):
            e = qa * sh_sc[0, r, 0:rows, :] * jnp.exp2(ba - sh_sc[1, r, 0:rows, :])
            term = jnp.dot(e.astype(BF16), ones_bd, preferred_element_type=F32) * sh_sc[2, r, 0:rows, :]
            acc = term if acc is None else acc + term
        o_ref[0, lo:, :] += acc

    b_last = b[chunk - 1:chunk, :]
    upd = _dot_tn(v, k * jnp.exp2(b_last - b))
    st_sc[...] = st_sc[...] * jnp.exp2(b_last) + jnp.where(_block_mask(dh, dk), upd, 0.0)

    @pl.when(ci == pl.num_programs(1) - 1)
    def _():
        sT_ref[0] = st_sc[...]


def _ret_kernel(x_ref, s0_ref, o_ref, sT_ref, st_sc, *, dh, dk, chunk, n_valid, log_gamma):
    ci = pl.program_id(1)

    @pl.when(ci == 0)
    def _():
        st_sc[...] = s0_ref[0]

    q = x_ref[0, :, 0:dh]
    k = x_ref[0, :, dh:2 * dh]
    v = x_ref[0, :, 2 * dh:3 * dh]
    lane = lax.broadcasted_iota(jnp.int32, (chunk, dh), 1)
    head = lane // dk
    lg = jnp.zeros((chunk, dh), F32)
    for h, val in enumerate(log_gamma):
        lg = jnp.where(head == h, val, lg)
    t = lax.broadcasted_iota(jnp.int32, (chunk, dh), 0).astype(F32)
    r = lax.broadcasted_iota(jnp.int32, (chunk, chunk), 0)
    c = lax.broadcasted_iota(jnp.int32, (chunk, chunk), 1)
    diff = (r - c).astype(F32)

    o = _dot_nt(q, st_sc[...]) * jnp.exp((t + 1.0) * lg)
    for h, val in enumerate(log_gamma):
        in_h = head == h
        a = _dot_nt(jnp.where(in_h, q, 0.0), k)
        a = a * jnp.where(diff >= 0, jnp.exp(jnp.maximum(diff, 0.0) * val), 0.0)
        o = o + jnp.where(in_h, _dot(a, v), 0.0)
    o_ref[0] = o

    upd = _dot_tn(v, k * jnp.exp((n_valid - 1.0 - t) * lg))
    st_sc[...] = st_sc[...] * jnp.exp(n_valid * lg[0:1, :]) + jnp.where(_block_mask(dh, dk), upd, 0.0)

    @pl.when(ci == pl.num_programs(1) - 1)
    def _():
        sT_ref[0] = st_sc[...]


def _recurrence(kind, x, s0, *, dh, dk, chunk, shared_state):
    b, t, w = x.shape
    n_valid = min(chunk, t)
    if t < chunk:
        x = jnp.pad(x, ((0, 0), (0, chunk - t), (0, 0)))
    t_pad = x.shape[1]
    s0_map = (lambda bi, ci: (0, 0, 0)) if shared_state else (lambda bi, ci: (bi, 0, 0))
    if kind == "hgrn":
        body = functools.partial(_hgrn_kernel, dh=dh, dk=dk, chunk=chunk)
        scratch = [pltpu.VMEM((dh, dh), F32), pltpu.VMEM((3, SUBLANES + chunk, dh), F32),
                   pltpu.VMEM((3, SUBLANES, chunk, dh), F32)]
    else:
        log_gamma = tuple(float(np.log1p(-np.exp2(-5.0 - h))) for h in range(dh // dk))
        body = functools.partial(_ret_kernel, dh=dh, dk=dk, chunk=chunk, n_valid=n_valid, log_gamma=log_gamma)
        scratch = [pltpu.VMEM((dh, dh), F32)]
    return pl.pallas_call(
        body,
        grid=(b, t_pad // chunk),
        in_specs=[pl.BlockSpec((1, chunk, w), lambda bi, ci: (bi, ci, 0)), pl.BlockSpec((1, dh, dh), s0_map)],
        out_specs=[pl.BlockSpec((1, chunk, dh), lambda bi, ci: (bi, ci, 0)),
                   pl.BlockSpec((1, dh, dh), lambda bi, ci: (bi, 0, 0))],
        out_shape=[jax.ShapeDtypeStruct((b, t_pad, dh), F32), jax.ShapeDtypeStruct((b, dh, dh), F32)],
        scratch_shapes=scratch,
        compiler_params=_params("parallel", "arbitrary"),
        name=kind,
    )(x, s0)


def _merge_kernel(om_ref, oh_ref, hg_ref, or_ref, rg_ref, h_ref, w_ref, gh_ref, gr_ref, o_ref, *, nm, dh, dv):
    def gated(o_r, gate_r, g_r):
        o = o_r[...]
        gate = gate_r[...]
        out = []
        for s in range(dh // LANES):
            x = o[:, s * LANES:(s + 1) * LANES]
            lane = lax.broadcasted_iota(jnp.int32, x.shape, 1)
            low = lane < dv
            x2 = x * x
            ss_l = jnp.sum(jnp.where(low, x2, 0.0), axis=-1, keepdims=True)
            ss_h = jnp.sum(jnp.where(low, 0.0, x2), axis=-1, keepdims=True)
            inv = lax.rsqrt(jnp.where(low, ss_l, ss_h) * (1.0 / dv) + EPS)
            gt = gate[:, s * LANES:(s + 1) * LANES]
            out.append(x * inv * g_r[...] * (gt * _sigmoid(gt)))
        return jnp.concatenate(out, axis=-1)

    y = _dot(om_ref[...], w_ref[0:nm, :])
    y = y + _dot(gated(oh_ref, hg_ref, gh_ref), w_ref[nm:nm + dh, :])
    y = y + _dot(gated(or_ref, rg_ref, gr_ref), w_ref[nm + dh:nm + 2 * dh, :])
    o_ref[...] = h_ref[...] + y


def _merge(o_mla, o_hg, hg_pack, o_rt, rt_pack, h, w_o, gh, gr, *, dh, dv):
    n, d = h.shape
    nm = o_mla.shape[1]
    tm = _row_tile(n, 512)
    row = lambda width, cb=0: pl.BlockSpec((tm, width), lambda i: (i, cb))
    const = lambda a: pl.BlockSpec(a.shape, lambda i: (0, 0))
    return pl.pallas_call(
        functools.partial(_merge_kernel, nm=nm, dh=dh, dv=dv),
        grid=(n // tm,),
        in_specs=[row(nm), row(dh), row(dh, 4), row(dh), row(dh, 3), row(d), const(w_o), const(gh), const(gr)],
        out_specs=row(d),
        out_shape=jax.ShapeDtypeStruct((n, d), F32),
        compiler_params=_params("parallel"),
        name="merge",
    )(o_mla, o_hg, hg_pack, o_rt, rt_pack, h, w_o, gh, gr)


def _ffn_kernel(h_ref, g_ref, wg_ref, wu_ref, wd_ref, o_ref, xn_sc, acc_sc):
    j = pl.program_id(1)

    @pl.when(j == 0)
    def _():
        x = h_ref[...]
        xn_sc[...] = (_rms(x, x.shape[-1]) * g_ref[...]).astype(BF16)
        acc_sc[...] = jnp.zeros(acc_sc.shape, F32)

    xn = xn_sc[...]
    a = jnp.dot(xn, wg_ref[...], preferred_element_type=F32)
    u = jnp.dot(xn, wu_ref[...], preferred_element_type=F32)
    acc_sc[...] += jnp.dot((a * _sigmoid(a) * u).astype(BF16), wd_ref[...], preferred_element_type=F32)

    @pl.when(j == pl.num_programs(1) - 1)
    def _():
        o_ref[...] = h_ref[...] + acc_sc[...]


def _ffn(h, g, wg, wu, wd):
    n, d = h.shape
    ff = wg.shape[1]
    tm = _row_tile(n, 512)
    tf = ff // 2 if (ff // 2) % LANES == 0 else ff
    return pl.pallas_call(
        _ffn_kernel,
        grid=(n // tm, ff // tf),
        in_specs=[pl.BlockSpec((tm, d), lambda i, j: (i, 0)), pl.BlockSpec(g.shape, lambda i, j: (0, 0)),
                  pl.BlockSpec((d, tf), lambda i, j: (0, j)), pl.BlockSpec((d, tf), lambda i, j: (0, j)),
                  pl.BlockSpec((tf, d), lambda i, j: (j, 0))],
        out_specs=pl.BlockSpec((tm, d), lambda i, j: (i, 0)),
        out_shape=jax.ShapeDtypeStruct((n, d), F32),
        scratch_shapes=[pltpu.VMEM((tm, d), BF16), pltpu.VMEM((tm, d), F32)],
        compiler_params=_params("parallel", "arbitrary"),
        name="ffn",
    )(h, g, wg, wu, wd)


def _moe_kernel(h_ref, g_ref, wr_ref, wg_ref, wu_ref, wd_ref, o_ref, xn_sc, gate_sc, acc_sc, *, ne):
    e = pl.program_id(1)

    @pl.when(e == 0)
    def _():
        x = h_ref[...]
        xn = _rms(x, x.shape[-1]) * g_ref[...]
        xn_sc[...] = xn.astype(BF16)
        acc_sc[...] = jnp.zeros(acc_sc.shape, F32)
        xh, xm, _ = _split3(xn)
        wh, wm, _ = _split3(wr_ref[...])
        dd = lambda a, b: jnp.dot(a, b, preferred_element_type=F32)
        logits = dd(xh, wh) + (dd(xh, wm) + dd(xm, wh))
        lane = lax.broadcasted_iota(jnp.int32, logits.shape, 1)
        valid = lane < ne
        lg = jnp.where(valid, logits, -jnp.inf)
        v1 = jnp.max(lg, axis=-1, keepdims=True)
        i1 = jnp.min(jnp.where(lg == v1, lane, LANES), axis=-1, keepdims=True)
        lg2 = jnp.where(lane == i1, -jnp.inf, lg)
        v2 = jnp.max(lg2, axis=-1, keepdims=True)
        i2 = jnp.min(jnp.where(lg2 == v2, lane, LANES), axis=-1, keepdims=True)
        e2 = jnp.exp(v2 - v1)
        w1 = 1.0 / (1.0 + e2)
        w2 = e2 / (1.0 + e2)
        gate_sc[...] = jnp.where(lane == i1, w1, jnp.where(lane == i2, w2, 0.0))

    lane = lax.broadcasted_iota(jnp.int32, gate_sc.shape, 1)
    gate = jnp.sum(jnp.where(lane == e, gate_sc[...], 0.0), axis=-1, keepdims=True)
    xn = xn_sc[...]
    a = jnp.dot(xn, wg_ref[0], preferred_element_type=F32)
    u = jnp.dot(xn, wu_ref[0], preferred_element_type=F32)
    acc_sc[...] += gate * jnp.dot((a * _sigmoid(a) * u).astype(BF16), wd_ref[0], preferred_element_type=F32)

    @pl.when(e == ne - 1)
    def _():
        o_ref[...] = h_ref[...] + acc_sc[...]


def _moe(h, g, wr, wg, wu, wd):
    n, d = h.shape
    ne, _, ff = wg.shape
    tm = _row_tile(n, 512)
    return pl.pallas_call(
        functools.partial(_moe_kernel, ne=ne),
        grid=(n // tm, ne),
        in_specs=[pl.BlockSpec((tm, d), lambda i, e: (i, 0)), pl.BlockSpec(g.shape, lambda i, e: (0, 0)),
                  pl.BlockSpec(wr.shape, lambda i, e: (0, 0)),
                  pl.BlockSpec((1, d, ff), lambda i, e: (e, 0, 0)), pl.BlockSpec((1, d, ff), lambda i, e: (e, 0, 0)),
                  pl.BlockSpec((1, ff, d), lambda i, e: (e, 0, 0))],
        out_specs=pl.BlockSpec((tm, d), lambda i, e: (i, 0)),
        out_shape=jax.ShapeDtypeStruct((n, d), F32),
        scratch_shapes=[pltpu.VMEM((tm, d), BF16), pltpu.VMEM((tm, LANES), F32), pltpu.VMEM((tm, d), F32)],
        compiler_params=_params("parallel", "arbitrary"),
        name="moe",
    )(h, g, wr, wg, wu, wd)


def _rope_tables(pos, rope_m, ret_dk):
    def angles(d):
        inv = ROPE_BASE ** (-jnp.arange(0, d, 2, dtype=F32) / d)
        return pos.astype(F32)[:, None] * inv[None, :]

    n = pos.shape[0]
    a = angles(rope_m)
    c, s = jnp.cos(a), jnp.sin(a)
    pad = LANES - 64 - rope_m
    cm = jnp.concatenate([jnp.ones((n, 64), F32), c, c, jnp.ones((n, pad), F32)], axis=1)
    sm = jnp.concatenate([jnp.zeros((n, 64), F32), -s, s, jnp.zeros((n, pad), F32)], axis=1)
    a = angles(ret_dk)
    c, s = jnp.cos(a), jnp.sin(a)
    reps = LANES // ret_dk
    cr = jnp.concatenate([c, c] * reps, axis=1)
    sr = jnp.concatenate([-s, s] * reps, axis=1)
    return cm, sm, cr, sr


def _slab_cols(w, nope, rope_m):
    r, nh, dd = w.shape
    return jnp.pad(w, ((0, 0), (0, 0), (0, LANES - dd))).reshape(r, nh * LANES)


def _block_diag_t(s):
    b, nh, dk, dv = s.shape
    eye = jnp.eye(nh, dtype=s.dtype)
    return jnp.einsum('bhkv,hg->bhvgk', s, eye).reshape(b, nh * dv, nh * dk)


def _unblock_t(st, nh):
    b, n, _ = st.shape
    dd = n // nh
    blocks = st.reshape(b, nh, dd, nh, dd)
    return jnp.stack([blocks[:, h, :, h, :] for h in range(nh)], axis=1).swapaxes(-1, -2)


def kernel(x_prompt, x_sample, cache_ckv, cache_krope, page_table, state_hgrn, state_ret, meta_tokens,
           norm_mix_g, w_in, q_lora_g, kv_lora_g, w_uq, w_uk, w_uv, qn_nope_g, qn_rope_g, kn_nope_g, kn_rope_g,
           hg_lb, hg_norm_g, ret_norm_g, w_o, norm_ffn_g, ffn_w_gate, ffn_w_up, ffn_w_down,
           moe_router, moe_w_gate, moe_w_up, moe_w_down):
    depth = w_in.shape[0]
    B, S, D = x_prompt.shape
    Bd, Sd, _ = x_sample.shape
    n_meta = meta_tokens.shape[0]
    nq, nkv = q_lora_g.shape[1], kv_lora_g.shape[1]
    nh, nope, rope_m = w_uq.shape[2], qn_nope_g.shape[1], qn_rope_g.shape[1]
    vdim = w_uv.shape[3]
    hg_heads, hg_dk = state_hgrn.shape[2], state_hgrn.shape[3]
    rt_heads, rt_dk = state_ret.shape[2], state_ret.shape[3]
    dh = hg_heads * hg_dk
    assert nope == 64 and rope_m == 32 and vdim == 64 and hg_dk == 64 and rt_dk == 64
    assert dh == rt_heads * rt_dk and nh % 2 == 0
    past_len = page_table.shape[1] * cache_ckv.shape[2]
    scale = float((nope + rope_m) ** -0.5 * LOG2E)
    ns = Bd * Sd

    h_b = x_prompt.reshape(B * S, D)
    h_s = jnp.concatenate([x_sample.reshape(ns, D), meta_tokens.astype(x_prompt.dtype)], axis=0)
    pos_b = jnp.tile(jnp.arange(S, dtype=F32) + n_meta, B)
    pos_s = jnp.concatenate([jnp.tile(jnp.arange(Sd, dtype=F32) + past_len, Bd), jnp.arange(n_meta, dtype=F32)])
    tabs_b = _rope_tables(pos_b, rope_m, rt_dk)
    tabs_s = _rope_tables(pos_s, rope_m, rt_dk)

    lb_soft = jax.nn.softmax(hg_lb.astype(F32), axis=0)
    lb_all = jnp.cumsum(lb_soft, axis=0) - lb_soft[0]

    lane_pad = lambda g, lo: jnp.pad(g, (lo, LANES - lo - g.shape[0]))[None, :]
    sel = jnp.zeros((LANES, rope_m), BF16).at[jnp.arange(rope_m) + 64, jnp.arange(rope_m)].set(1.0)
    cache_kropet = jnp.swapaxes(cache_krope, 2, 3)

    outs = {k: [] for k in ("ckv_p", "kr_p", "hg_p", "rt_p", "ckv_s", "kr_s", "hg_s", "rt_s")}
    for l in range(depth):
        wi = w_in[l]
        o_kr = nq + nkv
        w_perm = jnp.concatenate(
            [wi[:, :o_kr], jnp.zeros((D, 64), F32), wi[:, o_kr:o_kr + rope_m], jnp.zeros((D, LANES - 64 - rope_m), F32),
             wi[:, o_kr + rope_m:]], axis=1).astype(BF16)
        wq = _slab_cols(w_uq[l], nope, rope_m).astype(BF16)
        wk = _slab_cols(w_uk[l], nope, rope_m).astype(BF16)
        wvt = w_uv[l].reshape(nkv, nh * vdim).T.astype(BF16)
        wukt = w_uk[l].reshape(nkv, nh * nope).T.astype(BF16)
        wabs_w = jnp.einsum('rhd,hg->hdgr', w_uk[l], jnp.eye(nh, dtype=F32))
        wabs_w = jnp.pad(wabs_w, ((0, 0), (0, LANES - nope), (0, 0), (0, 0))).reshape(nh * LANES, nh * nkv).astype(BF16)
        wuv_bd = jnp.einsum('rhd,hg->hrgd', w_uv[l], jnp.eye(nh, dtype=F32)).reshape(nh * nkv, nh * vdim).astype(BF16)
        g_q = (lane_pad(qn_nope_g[l], 0) + lane_pad(qn_rope_g[l], 64)).astype(F32)
        g_k = lane_pad(kn_nope_g[l], 0).astype(F32)
        g_kr = lane_pad(kn_rope_g[l], 64).astype(F32)
        g_hg = jnp.tile(hg_norm_g[l], LANES // hg_dk)[None, :]
        g_rt = jnp.tile(ret_norm_g[l], LANES // rt_dk)[None, :]
        wo = w_o[l].astype(BF16)
        inproj = functools.partial(_inproj, gn=norm_mix_g[l][None, :], w=w_perm, gq=q_lora_g[l][None, :],
                                   gkv=kv_lora_g[l][None, :], gkr=g_kr, lb=lb_all[l][None, :],
                                   nq=nq, nkv=nkv, rope_m=rope_m, dh=dh)
        qprep = functools.partial(_qprep, w=wq, g=g_q, gk=g_k, nh=nh, nope=nope, rope_m=rope_m, scale=scale)

        cq_b, ckv_b, krs_b, hg_b, rt_b = inproj(h_b, tabs=tabs_b)
        cq_s, ckv_s, krs_s, hg_s, rt_s = inproj(h_s, tabs=tabs_s)
        (q_b,) = qprep(cq_b, cm=tabs_b[0], sm=tabs_b[1], with_qg=False)
        q_s, qg_s = qprep(cq_s, cm=tabs_s[0], sm=tabs_s[1], with_qg=True)
        k_b, vt_b = _kvup(ckv_b, krs_b, wk, wvt, g_k, nh=nh, nope=nope)
        k_m, vt_m = _kvup(ckv_s[ns:], krs_s[ns:], wk, wvt, g_k, nh=nh, nope=nope)

        o_b = _flash(q_b, k_b, vt_b, (k_m, vt_m), nh=nh, batch=B)
        mpad = LANES - n_meta
        o_m = _flash(jnp.pad(q_s[ns:], ((0, mpad), (0, 0))), jnp.pad(k_m, ((0, mpad), (0, 0))),
                     jnp.pad(vt_m, ((0, 0), (0, mpad))), None, nh=nh, batch=1)[:n_meta]
        wabs = _matmul(qg_s[:ns], wabs_w, BF16, "qabsorb").reshape(Bd, Sd * nh, nkv)
        pad_new = lambda a: jnp.pad(a.reshape(Bd, Sd, -1), ((0, 0), (0, LANES - Sd), (0, 0)))
        kr_s_compact = krs_s[:, 64:64 + rope_m]
        acc = _paged_attention(page_table, q_s[:ns].reshape(Bd, Sd * nh, LANES), wabs, wukt, sel,
                               pad_new(ckv_s[:ns]), pad_new(kr_s_compact[:ns]).swapaxes(1, 2), cache_ckv, cache_kropet, l,
                               nh=nh, nope=nope, sd=Sd)
        o_s = _matmul(acc.reshape(ns, nh * nkv), wuv_bd, F32, "vabsorb")
        o_small = jnp.concatenate([o_s, o_m], axis=0)

        zero_state = jnp.zeros((1, dh, dh), F32)
        rec = functools.partial(_recurrence, dh=dh, dk=hg_dk)
        ohg_m, sT_hg_m = rec("hgrn", hg_s[ns:][None], zero_state, chunk=SMALL_CHUNK, shared_state=True)
        ohg_s, sT_hg_s = rec("hgrn", hg_s[:ns].reshape(Bd, Sd, -1), _block_diag_t(state_hgrn[l].astype(F32)),
                             chunk=SMALL_CHUNK, shared_state=False)
        ohg_b, sT_hg_b = rec("hgrn", hg_b.reshape(B, S, -1), sT_hg_m, chunk=HG_CHUNK, shared_state=True)
        ort_m, sT_rt_m = rec("ret", rt_s[ns:][None], zero_state, chunk=SMALL_CHUNK, shared_state=True)
        ort_s, sT_rt_s = rec("ret", rt_s[:ns].reshape(Bd, Sd, -1), _block_diag_t(state_ret[l].astype(F32)),
                             chunk=SMALL_CHUNK, shared_state=False)
        ort_b, sT_rt_b = rec("ret", rt_b.reshape(B, S, -1), sT_rt_m, chunk=RET_CHUNK, shared_state=True)
        ohg_small = jnp.concatenate([ohg_s[:, :Sd].reshape(ns, dh), ohg_m[0, :n_meta]], axis=0)
        ort_small = jnp.concatenate([ort_s[:, :Sd].reshape(ns, dh), ort_m[0, :n_meta]], axis=0)

        merge = functools.partial(_merge, w_o=wo, gh=g_hg, gr=g_rt, dh=dh, dv=hg_dk)
        h_b = merge(o_b, ohg_b.reshape(B * S, dh), hg_b, ort_b.reshape(B * S, dh), rt_b, h_b)
        h_s = merge(o_small, ohg_small, hg_s, ort_small, rt_s, h_s)
        g_ffn = norm_ffn_g[l][None, :]
        i = l // 2
        if l % 2 == 0:
            mix = functools.partial(_ffn, g=g_ffn, wg=ffn_w_gate[i].astype(BF16), wu=ffn_w_up[i].astype(BF16),
                                    wd=ffn_w_down[i].astype(BF16))
        else:
            ne = moe_router.shape[2]
            wr = jnp.pad(moe_router[i], ((0, 0), (0, LANES - ne)))
            mix = functools.partial(_moe, g=g_ffn, wr=wr, wg=moe_w_gate[i].astype(BF16), wu=moe_w_up[i].astype(BF16),
                                    wd=moe_w_down[i].astype(BF16))
        h_b = mix(h_b)
        h_s = mix(h_s)

        with_meta = lambda real, meta: jnp.concatenate(
            [jnp.broadcast_to(meta[None], (B,) + meta.shape), real.reshape(B, S, -1)], axis=1)
        outs["ckv_p"].append(with_meta(ckv_b, ckv_s[ns:]))
        outs["kr_p"].append(with_meta(krs_b[:, 64:64 + rope_m], kr_s_compact[ns:]))
        outs["hg_p"].append(_unblock_t(sT_hg_b, hg_heads).astype(state_hgrn.dtype))
        outs["rt_p"].append(_unblock_t(sT_rt_b, rt_heads).astype(state_ret.dtype))
        outs["ckv_s"].append(ckv_s[:ns].reshape(Bd, Sd, nkv))
        outs["kr_s"].append(kr_s_compact[:ns].reshape(Bd, Sd, rope_m))
        outs["hg_s"].append(_unblock_t(sT_hg_s, hg_heads).astype(state_hgrn.dtype))
        outs["rt_s"].append(_unblock_t(sT_rt_s, rt_heads).astype(state_ret.dtype))

    y_prompt = h_b.reshape(B, S, D)
    y_sample = h_s[:ns].reshape(Bd, Sd, D)
    stack = lambda k: jnp.stack(outs[k])
    return (y_prompt, y_sample, stack("ckv_p"), stack("kr_p"), stack("hg_p"), stack("rt_p"),
            stack("ckv_s"), stack("kr_s"), stack("hg_s"), stack("rt_s"))
```

```python
import functools

import numpy as np
import jax
import jax.numpy as jnp
from jax import lax
from jax.experimental import pallas as pl
from jax.experimental.pallas import tpu as pltpu

F32 = jnp.float32
BF16 = jnp.bfloat16

EPS = 1e-6
ROPE_BASE = 10000.0
NEG_BIG = -1e30
LOG2E = 1.4426950408889634
LANES = 128
SUBLANES = 8
VMEM_LIMIT = 56 * 1024 * 1024

HG_CHUNK = 64
RET_CHUNK = 128
SMALL_CHUNK = 16
REC_BATCH_BLOCK = 8
PAGES_PER_STEP = 16
QUERY_SPLIT = 1
FLASH_TILE = 1024


def _params(*sem):
    return pltpu.CompilerParams(dimension_semantics=sem, vmem_limit_bytes=VMEM_LIMIT)


def _row_tile(n, pref):
    if n <= pref:
        return n
    t = pref - pref % 16
    while t >= 16:
        if n % t == 0:
            return t
        t -= 16
    return n


def _dot(a, b):
    return jnp.dot(a.astype(BF16), b.astype(BF16), preferred_element_type=F32)


def _dot_nt(a, b):
    return lax.dot_general(a.astype(BF16), b.astype(BF16), (((1,), (1,)), ((), ())), preferred_element_type=F32)


def _dot_tn(a, b):
    return lax.dot_general(a.astype(BF16), b.astype(BF16), (((0,), (0,)), ((), ())), preferred_element_type=F32)


def _rms(x, width):
    return x * lax.rsqrt(jnp.sum(x * x, axis=-1, keepdims=True) * (1.0 / width) + EPS)


def _rope128(y, cos, sin_signed, half):
    lane = lax.broadcasted_iota(jnp.int32, y.shape, 1)
    first = (lane % (2 * half)) < half
    rot = jnp.where(first, pltpu.roll(y, LANES - half, 1), pltpu.roll(y, half, 1))
    return y * cos + rot * sin_signed


def _sigmoid(x):
    return 1.0 / (1.0 + jnp.exp(-x))


def _inproj_kernel(h_ref, gn_ref, w_ref, gq_ref, gkv_ref, gkr_ref, lb_ref, cm_ref, sm_ref, cr_ref, sr_ref,
                   cq_ref, ckv_ref, kr_ref, hg_ref, rt_ref, *, nq, nkv, rope_m, dh):
    x = h_ref[...]
    xb = (_rms(x, x.shape[-1]) * gn_ref[...]).astype(BF16)

    def proj(lo, width):
        return jnp.dot(xb, w_ref[:, lo:lo + width], preferred_element_type=F32)

    c = proj(0, nq)
    cq_ref[...] = (_rms(c, nq) * gq_ref[...]).astype(BF16)
    c = proj(nq, nkv)
    ckv_ref[...] = _rms(c, nkv) * gkv_ref[...]
    k = proj(nq + nkv, LANES)
    k = _rms(k, rope_m) * gkr_ref[...]
    kr_ref[...] = _rope128(k, cm_ref[...], sm_ref[...], rope_m // 2)

    base = nq + nkv + LANES
    hq = proj(base, dh)
    hg_ref[:, 0:dh] = hq * _sigmoid(hq)
    z = proj(base + dh, dh)
    lb = lb_ref[...]
    f = lb + (1.0 - lb) * _sigmoid(z)
    hg_ref[:, dh:2 * dh] = 1.0 - f
    hg_ref[:, 2 * dh:3 * dh] = jnp.log(f)
    hg_ref[:, 3 * dh:4 * dh] = proj(base + 2 * dh, dh)
    hg_ref[:, 4 * dh:5 * dh] = proj(base + 3 * dh, dh)

    base = base + 4 * dh
    cr = cr_ref[...]
    sr = sr_ref[...]
    for part, scale in ((0, 1.0), (1, 64 ** -0.5)):
        for s in range(dh // LANES):
            lo = part * dh + s * LANES
            y = _rope128(proj(base + lo, LANES), cr, sr, 32)
            rt_ref[:, lo:lo + LANES] = y * scale
    rt_ref[:, 2 * dh:3 * dh] = proj(base + 2 * dh, dh)
    rt_ref[:, 3 * dh:4 * dh] = proj(base + 3 * dh, dh)


def _inproj(h, gn, w, gq, gkv, gkr, lb, tabs, *, nq, nkv, rope_m, dh):
    n, d = h.shape
    tm = _row_tile(n, 512)
    cm, sm, cr, sr = tabs
    row = lambda width: pl.BlockSpec((tm, width), lambda i: (i, 0))
    const = lambda a: pl.BlockSpec(a.shape, lambda i: (0, 0))
    return pl.pallas_call(
        functools.partial(_inproj_kernel, nq=nq, nkv=nkv, rope_m=rope_m, dh=dh),
        grid=(n // tm,),
        in_specs=[row(d), const(gn), const(w), const(gq), const(gkv), const(gkr), const(lb),
                  row(LANES), row(LANES), row(LANES), row(LANES)],
        out_specs=[row(nq), row(nkv), row(LANES), row(5 * dh), row(4 * dh)],
        out_shape=[jax.ShapeDtypeStruct((n, nq), BF16), jax.ShapeDtypeStruct((n, nkv), F32),
                   jax.ShapeDtypeStruct((n, LANES), F32), jax.ShapeDtypeStruct((n, 5 * dh), F32),
                   jax.ShapeDtypeStruct((n, 4 * dh), F32)],
        compiler_params=_params("parallel"),
        name="inproj",
    )(h, gn, w, gq, gkv, gkr, lb, cm, sm, cr, sr)


def _qprep_kernel(c_ref, w_ref, g_ref, gk_ref, cm_ref, sm_ref, *out_refs, nh, nope, rope_m, scale, with_qg):
    c = c_ref[...]
    cm = cm_ref[...]
    sm = sm_ref[...]
    g = g_ref[...]
    for h in range(nh):
        x = jnp.dot(c, w_ref[:, h * LANES:(h + 1) * LANES], preferred_element_type=F32)
        lane = lax.broadcasted_iota(jnp.int32, x.shape, 1)
        is_nope = lane < nope
        x2 = x * x
        ss_n = jnp.sum(jnp.where(is_nope, x2, 0.0), axis=-1, keepdims=True)
        ss_r = jnp.sum(jnp.where(is_nope, 0.0, x2), axis=-1, keepdims=True)
        inv = jnp.where(is_nope, lax.rsqrt(ss_n * (1.0 / nope) + EPS), lax.rsqrt(ss_r * (1.0 / rope_m) + EPS))
        y = _rope128(x * inv * g, cm, sm, rope_m // 2) * scale
        out_refs[0][:, h * LANES:(h + 1) * LANES] = y.astype(BF16)
        if with_qg:
            out_refs[1][:, h * LANES:(h + 1) * LANES] = (y * gk_ref[...]).astype(BF16)


def _qprep(cqn, w, g, gk, cm, sm, *, nh, nope, rope_m, scale, with_qg):
    n, r = cqn.shape
    tm = _row_tile(n, 512)
    row = lambda width: pl.BlockSpec((tm, width), lambda i: (i, 0))
    const = lambda a: pl.BlockSpec(a.shape, lambda i: (0, 0))
    nout = 2 if with_qg else 1
    return pl.pallas_call(
        functools.partial(_qprep_kernel, nh=nh, nope=nope, rope_m=rope_m, scale=scale, with_qg=with_qg),
        grid=(n // tm,),
        in_specs=[row(r), const(w), const(g), const(gk), row(LANES), row(LANES)],
        out_specs=[row(nh * LANES)] * nout,
        out_shape=[jax.ShapeDtypeStruct((n, nh * LANES), BF16)] * nout,
        compiler_params=_params("parallel"),
        name="qprep",
    )(cqn, w, g, gk, cm, sm)


def _kvup_kernel(c_ref, kr_ref, wk_ref, wvt_ref, gk_ref, k_ref, vt_ref, *, nh, nope):
    c = c_ref[...].astype(BF16)
    kr = kr_ref[...]
    gk = gk_ref[...]
    for h in range(nh):
        x = jnp.dot(c, wk_ref[:, h * LANES:(h + 1) * LANES], preferred_element_type=F32)
        k_ref[:, h * LANES:(h + 1) * LANES] = (_rms(x, nope) * gk + kr).astype(BF16)
    vt_ref[...] = _dot_nt(wvt_ref[...], c).astype(BF16)


def _kvup(ckv, krs, wk, wvt, gk, *, nh, nope):
    n, r = ckv.shape
    tm = _row_tile(n, 512)
    row = lambda width: pl.BlockSpec((tm, width), lambda i: (i, 0))
    const = lambda a: pl.BlockSpec(a.shape, lambda i: (0, 0))
    nv = wvt.shape[0]
    return pl.pallas_call(
        functools.partial(_kvup_kernel, nh=nh, nope=nope),
        grid=(n // tm,),
        in_specs=[row(r), row(LANES), const(wk), const(wvt), const(gk)],
        out_specs=[row(nh * LANES), pl.BlockSpec((nv, tm), lambda i: (0, i))],
        out_shape=[jax.ShapeDtypeStruct((n, nh * LANES), BF16), jax.ShapeDtypeStruct((nv, n), BF16)],
        compiler_params=_params("parallel"),
        name="kvup",
    )(ckv, krs, wk, wvt, gk)


def _flash_kernel(*refs, nh, vdim, has_prefix):
    if has_prefix:
        q_ref, k_ref, vt_ref, kp_ref, vtp_ref, o_ref, m_sc, l_sc, acc_sc = refs
    else:
        q_ref, k_ref, vt_ref, o_ref, m_sc, l_sc, acc_sc = refs
    qi = pl.program_id(1)
    ki = pl.program_id(2)

    def update(k_r, vt_r, masked):
        tq = q_ref.shape[0]
        ts = tq // QUERY_SPLIT if tq % (QUERY_SPLIT * LANES) == 0 else tq
        for h in range(nh):
            rows = slice(h * vdim, (h + 1) * vdim)
            for c in range(tq // ts):
                qs = slice(c * ts, (c + 1) * ts)
                st = _dot_nt(k_r[:, h * LANES:(h + 1) * LANES], q_ref[qs, h * LANES:(h + 1) * LANES])
                if masked:
                    key = lax.broadcasted_iota(jnp.int32, st.shape, 0)
                    qry = lax.broadcasted_iota(jnp.int32, st.shape, 1) + c * ts
                    st = jnp.where(key <= qry, st, NEG_BIG)
                m_prev = m_sc[h:h + 1, qs]
                m_new = jnp.maximum(m_prev, jnp.max(st, axis=0, keepdims=True))
                alpha = jnp.exp2(m_prev - m_new)
                e = jnp.exp2(st - m_new)
                l_sc[h:h + 1, qs] = alpha * l_sc[h:h + 1, qs] + jnp.sum(e, axis=0, keepdims=True)
                m_sc[h:h + 1, qs] = m_new
                pv = jnp.dot(vt_r[rows, :], e.astype(BF16), preferred_element_type=F32)
                acc_sc[rows, qs] = acc_sc[rows, qs] * alpha + pv

    @pl.when(ki == 0)
    def _():
        m_sc[...] = jnp.full(m_sc.shape, NEG_BIG, F32)
        l_sc[...] = jnp.zeros(l_sc.shape, F32)
        acc_sc[...] = jnp.zeros(acc_sc.shape, F32)
        if has_prefix:
            update(kp_ref, vtp_ref, False)

    @pl.when(ki < qi)
    def _():
        update(k_ref, vt_ref, False)

    @pl.when(ki == qi)
    def _():
        update(k_ref, vt_ref, True)
        for h in range(nh):
            rows = slice(h * vdim, (h + 1) * vdim)
            acc_sc[rows, :] = acc_sc[rows, :] / l_sc[h:h + 1, :]
        o_ref[...] = acc_sc[...].T


def _flash(q, k, vt, prefix, *, nh, batch):
    n = q.shape[0]
    t = n // batch
    dv = vt.shape[0]
    tq = _row_tile(t, FLASH_TILE)
    nt = t // tq
    qspec = pl.BlockSpec((tq, nh * LANES), lambda bi, qi, ki: (bi * nt + qi, 0))
    kspec = pl.BlockSpec((tq, nh * LANES), lambda bi, qi, ki: (bi * nt + jnp.minimum(ki, qi), 0))
    vspec = pl.BlockSpec((dv, tq), lambda bi, qi, ki: (0, bi * nt + jnp.minimum(ki, qi)))
    ins, specs = [q, k, vt], [qspec, kspec, vspec]
    if prefix is not None:
        for a in prefix:
            ins.append(a)
            specs.append(pl.BlockSpec(a.shape, lambda bi, qi, ki: (0, 0)))
    return pl.pallas_call(
        functools.partial(_flash_kernel, nh=nh, vdim=dv // nh, has_prefix=prefix is not None),
        grid=(batch, nt, nt),
        in_specs=specs,
        out_specs=pl.BlockSpec((tq, dv), lambda bi, qi, ki: (bi * nt + qi, 0)),
        out_shape=jax.ShapeDtypeStruct((n, dv), F32),
        scratch_shapes=[pltpu.VMEM((nh, tq), F32), pltpu.VMEM((nh, tq), F32), pltpu.VMEM((dv, tq), F32)],
        compiler_params=_params("parallel", "parallel", "arbitrary"),
        name="flash_prefix" if prefix is not None else "flash",
    )(*ins)


def _matmul_kernel(a_ref, w_ref, o_ref):
    o_ref[...] = jnp.dot(a_ref[...].astype(BF16), w_ref[...], preferred_element_type=F32).astype(o_ref.dtype)


def _matmul(a, w, out_dtype, name):
    n, k = a.shape
    tm = _row_tile(n, 512)
    return pl.pallas_call(
        _matmul_kernel,
        grid=(n // tm,),
        in_specs=[pl.BlockSpec((tm, k), lambda i: (i, 0)), pl.BlockSpec(w.shape, lambda i: (0, 0))],
        out_specs=pl.BlockSpec((tm, w.shape[1]), lambda i: (i, 0)),
        out_shape=jax.ShapeDtypeStruct((n, w.shape[1]), out_dtype),
        compiler_params=_params("parallel"),
        name=name,
    )(a, w)


def _paged_kernel(pt_ref, q_ref, wabs_ref, wukt_ref, sel_ref, cn_ref, krn_ref, ckv_hbm, krt_hbm, o_ref,
                  lhs_sc, cbuf, kbuf, sem, *, layer, npg, nchunk, sub, nh, nope, sd):
    b = pl.program_id(0)
    nb = pl.num_programs(0)
    nk = nh * nope
    rows = sd * nh
    page = ckv_hbm.shape[2]

    def page_copies(bb, c, p):
        slot = c % 2
        idx = pt_ref[bb, c * npg + p]
        lanes = pl.ds(p * page, page)
        return (pltpu.make_async_copy(ckv_hbm.at[layer, idx], cbuf.at[slot, lanes, :], sem.at[0, slot]),
                pltpu.make_async_copy(krt_hbm.at[layer, idx], kbuf.at[slot, :, lanes], sem.at[1, slot]))

    def start_chunk(bb, c):
        for p in range(npg):
            for cp in page_copies(bb, c, p):
                cp.start()

    def wait_chunk(bb, c):
        for p in range(npg):
            for cp in page_copies(bb, c, p):
                cp.wait()

    @pl.when(b == 0)
    def _():
        start_chunk(0, 0)

    lhs_sc[0:nk, :] = wukt_ref[...]
    lhs_sc[nk:nk + rows, :] = wabs_ref[0]
    qr = jnp.dot(q_ref[0], sel_ref[...], preferred_element_type=F32).astype(BF16)

    def scores(c, krt):
        big = _dot_nt(lhs_sc[...], c)
        kraw = big[0:nk]
        ssq = jnp.sum((kraw * kraw).reshape(nh, nope, kraw.shape[-1]), axis=1)
        r = lax.rsqrt(ssq * (1.0 / nope) + EPS)
        return big[nk:nk + rows] * jnp.concatenate([r] * sd, axis=0) + jnp.dot(qr, krt, preferred_element_type=F32)

    def update(state, s, c):
        m_prev, l_prev, acc = state
        m_new = jnp.maximum(m_prev, jnp.max(s, axis=-1, keepdims=True))
        alpha = jnp.exp2(m_prev - m_new)
        e = jnp.exp2(s - m_new)
        l_new = alpha * l_prev + jnp.sum(e, axis=-1, keepdims=True)
        return m_new, l_new, acc * alpha + jnp.dot(e.astype(BF16), c, preferred_element_type=F32)

    state = (jnp.full((rows, 1), NEG_BIG, F32), jnp.zeros((rows, 1), F32), jnp.zeros((rows, cbuf.shape[2]), F32))
    span = sub * page
    for c in range(nchunk):
        slot = c % 2
        if c + 1 < nchunk:
            start_chunk(b, c + 1)
        else:
            @pl.when(b + 1 < nb)
            def _():
                start_chunk(b + 1, 0)
        wait_chunk(b, c)
        cs, ss = [], []
        for g in range(npg // sub):
            lat = cbuf[slot, g * span:(g + 1) * span, :].astype(BF16)
            cs.append(lat)
            ss.append(scores(lat, kbuf[slot, :, g * span:(g + 1) * span].astype(BF16)))
        state = update(state, jnp.concatenate(ss, axis=1), jnp.concatenate(cs, axis=0))

    lat = cn_ref[0].astype(BF16)
    s = scores(lat, krn_ref[0].astype(BF16))
    row = lax.broadcasted_iota(jnp.int32, s.shape, 0)
    col = lax.broadcasted_iota(jnp.int32, s.shape, 1)
    _, l_fin, acc = update(state, jnp.where(col * nh <= row, s, NEG_BIG), lat)
    o_ref[0] = acc / l_fin


def _paged_attention(page_table, q, wabs, wukt, sel, cn, krnt, cache_ckv, cache_kropet, layer, *, nh, nope, sd):
    bd, rows, _ = q.shape
    n_pages = page_table.shape[1]
    assert n_pages % 2 == 0
    npg = max(p for p in range(1, PAGES_PER_STEP + 1) if n_pages % (2 * p) == 0)
    nchunk = n_pages // npg
    sub = 2 if npg % 2 == 0 else 1
    page, kvl = cache_ckv.shape[2:]
    rp = cache_kropet.shape[2]
    per_b = lambda a: pl.BlockSpec((1,) + a.shape[1:], lambda b, pt: (b, 0, 0))
    const = lambda a: pl.BlockSpec(a.shape, lambda b, pt: (0, 0))
    hbm = pl.BlockSpec(memory_space=pl.ANY)
    grid_spec = pltpu.PrefetchScalarGridSpec(
        num_scalar_prefetch=1,
        grid=(bd,),
        in_specs=[per_b(q), per_b(wabs), const(wukt), const(sel), per_b(cn), per_b(krnt), hbm, hbm],
        out_specs=pl.BlockSpec((1, rows, kvl), lambda b, pt: (b, 0, 0)),
        scratch_shapes=[pltpu.VMEM((nh * nope + rows, kvl), BF16), pltpu.VMEM((2, npg * page, kvl), F32),
                        pltpu.VMEM((2, rp, npg * page), F32), pltpu.SemaphoreType.DMA((2, 2))],
    )
    return pl.pallas_call(
        functools.partial(_paged_kernel, layer=layer, npg=npg, nchunk=nchunk, sub=sub, nh=nh, nope=nope, sd=sd),
        grid_spec=grid_spec,
        out_shape=jax.ShapeDtypeStruct((bd, rows, kvl), F32),
        compiler_params=_params("arbitrary"),
        name="paged_attention",
    )(page_table, q, wabs, wukt, sel, cn, krnt, cache_ckv, cache_kropet)


def _block_mask(n, blk):
    r = lax.broadcasted_iota(jnp.int32, (n, n), 0) // blk
    c = lax.broadcasted_iota(jnp.int32, (n, n), 1) // blk
    return r == c


def _split3(x):
    hi = x.astype(BF16)
    r1 = x - hi.astype(F32)
    mid = r1.astype(BF16)
    lo = (r1 - mid.astype(F32)).astype(BF16)
    return hi, mid, lo


def _hgrn_kernel(x_ref, s0_ref, o_ref, sT_ref, st_sc, pad_sc, sh_sc, *, dh, dk, chunk, bb):
    for i in range(bb):
        _hgrn_one(x_ref.at[i], s0_ref.at[i % s0_ref.shape[0]], o_ref.at[i], sT_ref.at[i],
                  st_sc.at[i], pad_sc.at[i], sh_sc.at[i], dh=dh, dk=dk, chunk=chunk)


def _hgrn_one(x_ref, s0_ref, o_ref, sT_ref, st_sc, pad_sc, sh_sc, *, dh, dk, chunk):
    ci = pl.program_id(1)

    @pl.when(ci == 0)
    def _():
        st_sc[...] = s0_ref[...]
        pad_sc[:, 0:SUBLANES, :] = jnp.zeros((3, SUBLANES, dh), F32)

    q = x_ref[:, 0:dh]
    k = x_ref[:, dh:2 * dh]
    g = x_ref[:, 2 * dh:3 * dh]
    v = x_ref[:, 3 * dh:4 * dh]
    r = lax.broadcasted_iota(jnp.int32, (chunk, chunk), 0)
    c = lax.broadcasted_iota(jnp.int32, (chunk, chunk), 1)
    tri = jnp.where(c <= r, 1.0, 0.0).astype(BF16)
    b = sum(jnp.dot(tri, part, preferred_element_type=F32) for part in _split3(g * LOG2E))
    for i, val in enumerate((k, b, v)):
        pad_sc[i, SUBLANES:SUBLANES + chunk, :] = val
        sh_sc[i, 0] = val
        for r in range(1, SUBLANES):
            sh_sc[i, r] = pad_sc[i, SUBLANES - r:SUBLANES - r + chunk, :]
    ones_bd = jnp.where(_block_mask(dh, dk), 1.0, 0.0).astype(BF16)

    o_ref[...] = _dot_nt(q * jnp.exp2(b), st_sc[...])
    for a in range(chunk // SUBLANES):
        lo = a * SUBLANES
        rows = chunk - lo
        qa = q[lo:, :]
        ba = b[lo:, :]
        acc = None
        for r in range(SUBLANES):
            e = qa * sh_sc[0, r, 0:rows, :] * jnp.exp2(ba - sh_sc[1, r, 0:rows, :])
            term = jnp.dot(e.astype(BF16), ones_bd, preferred_element_type=F32) * sh_sc[2, r, 0:rows, :]
            acc = term if acc is None else acc + term
        o_ref[lo:, :] += acc

    b_last = b[chunk - 1:chunk, :]
    upd = _dot_tn(v, k * jnp.exp2(b_last - b))
    st_sc[...] = st_sc[...] * jnp.exp2(b_last) + jnp.where(_block_mask(dh, dk), upd, 0.0)

    @pl.when(ci == pl.num_programs(1) - 1)
    def _():
        sT_ref[...] = st_sc[...]


def _ret_kernel(x_ref, s0_ref, o_ref, sT_ref, st_sc, *, bb, **kw):
    for i in range(bb):
        _ret_one(x_ref.at[i], s0_ref.at[i % s0_ref.shape[0]], o_ref.at[i], sT_ref.at[i], st_sc.at[i], **kw)


def _ret_one(x_ref, s0_ref, o_ref, sT_ref, st_sc, *, dh, dk, chunk, n_valid, log_gamma):
    ci = pl.program_id(1)

    @pl.when(ci == 0)
    def _():
        st_sc[...] = s0_ref[...]

    q = x_ref[:, 0:dh]
    k = x_ref[:, dh:2 * dh]
    v = x_ref[:, 2 * dh:3 * dh]
    lane = lax.broadcasted_iota(jnp.int32, (chunk, dh), 1)
    head = lane // dk
    lg = jnp.zeros((chunk, dh), F32)
    for h, val in enumerate(log_gamma):
        lg = jnp.where(head == h, val, lg)
    t = lax.broadcasted_iota(jnp.int32, (chunk, dh), 0).astype(F32)
    r = lax.broadcasted_iota(jnp.int32, (chunk, chunk), 0)
    c = lax.broadcasted_iota(jnp.int32, (chunk, chunk), 1)
    diff = (r - c).astype(F32)

    o = _dot_nt(q, st_sc[...]) * jnp.exp((t + 1.0) * lg)
    for h, val in enumerate(log_gamma):
        in_h = head == h
        a = _dot_nt(jnp.where(in_h, q, 0.0), k)
        a = a * jnp.where(diff >= 0, jnp.exp(jnp.maximum(diff, 0.0) * val), 0.0)
        o = o + jnp.where(in_h, _dot(a, v), 0.0)
    o_ref[...] = o

    upd = _dot_tn(v, k * jnp.exp((n_valid - 1.0 - t) * lg))
    st_sc[...] = st_sc[...] * jnp.exp(n_valid * lg[0:1, :]) + jnp.where(_block_mask(dh, dk), upd, 0.0)

    @pl.when(ci == pl.num_programs(1) - 1)
    def _():
        sT_ref[...] = st_sc[...]


def _recurrence(kind, x, s0, *, dh, dk, chunk, shared_state):
    b, t, w = x.shape
    n_valid = min(chunk, t)
    if t < chunk:
        x = jnp.pad(x, ((0, 0), (0, chunk - t), (0, 0)))
    t_pad = x.shape[1]
    bb = max(d for d in range(1, REC_BATCH_BLOCK + 1) if b % d == 0)
    s0_spec = (pl.BlockSpec((1, dh, dh), lambda bi, ci: (0, 0, 0)) if shared_state
               else pl.BlockSpec((bb, dh, dh), lambda bi, ci: (bi, 0, 0)))
    if kind == "hgrn":
        body = functools.partial(_hgrn_kernel, dh=dh, dk=dk, chunk=chunk, bb=bb)
        scratch = [pltpu.VMEM((bb, dh, dh), F32), pltpu.VMEM((bb, 3, SUBLANES + chunk, dh), F32),
                   pltpu.VMEM((bb, 3, SUBLANES, chunk, dh), F32)]
    else:
        log_gamma = tuple(float(np.log1p(-np.exp2(-5.0 - h))) for h in range(dh // dk))
        body = functools.partial(_ret_kernel, bb=bb, dh=dh, dk=dk, chunk=chunk, n_valid=n_valid,
                                 log_gamma=log_gamma)
        scratch = [pltpu.VMEM((bb, dh, dh), F32)]
    return pl.pallas_call(
        body,
        grid=(b // bb, t_pad // chunk),
        in_specs=[pl.BlockSpec((bb, chunk, w), lambda bi, ci: (bi, ci, 0)), s0_spec],
        out_specs=[pl.BlockSpec((bb, chunk, dh), lambda bi, ci: (bi, ci, 0)),
                   pl.BlockSpec((bb, dh, dh), lambda bi, ci: (bi, 0, 0))],
        out_shape=[jax.ShapeDtypeStruct((b, t_pad, dh), F32), jax.ShapeDtypeStruct((b, dh, dh), F32)],
        scratch_shapes=scratch,
        compiler_params=_params("parallel", "arbitrary"),
        name=kind,
    )(x, s0)


def _merge_kernel(om_ref, oh_ref, hg_ref, or_ref, rg_ref, h_ref, w_ref, gh_ref, gr_ref, o_ref, *, nm, dh, dv):
    def gated(o_r, gate_r, g_r):
        o = o_r[...]
        gate = gate_r[...]
        out = []
        for s in range(dh // LANES):
            x = o[:, s * LANES:(s + 1) * LANES]
            lane = lax.broadcasted_iota(jnp.int32, x.shape, 1)
            low = lane < dv
            x2 = x * x
            ss_l = jnp.sum(jnp.where(low, x2, 0.0), axis=-1, keepdims=True)
            ss_h = jnp.sum(jnp.where(low, 0.0, x2), axis=-1, keepdims=True)
            inv = lax.rsqrt(jnp.where(low, ss_l, ss_h) * (1.0 / dv) + EPS)
            gt = gate[:, s * LANES:(s + 1) * LANES]
            out.append(x * inv * g_r[...] * (gt * _sigmoid(gt)))
        return jnp.concatenate(out, axis=-1)

    y = _dot(om_ref[...], w_ref[0:nm, :])
    y = y + _dot(gated(oh_ref, hg_ref, gh_ref), w_ref[nm:nm + dh, :])
    y = y + _dot(gated(or_ref, rg_ref, gr_ref), w_ref[nm + dh:nm + 2 * dh, :])
    o_ref[...] = h_ref[...] + y


def _merge(o_mla, o_hg, hg_pack, o_rt, rt_pack, h, w_o, gh, gr, *, dh, dv):
    n, d = h.shape
    nm = o_mla.shape[1]
    tm = _row_tile(n, 512)
    row = lambda width, cb=0: pl.BlockSpec((tm, width), lambda i: (i, cb))
    const = lambda a: pl.BlockSpec(a.shape, lambda i: (0, 0))
    return pl.pallas_call(
        functools.partial(_merge_kernel, nm=nm, dh=dh, dv=dv),
        grid=(n // tm,),
        in_specs=[row(nm), row(dh), row(dh, 4), row(dh), row(dh, 3), row(d), const(w_o), const(gh), const(gr)],
        out_specs=row(d),
        out_shape=jax.ShapeDtypeStruct((n, d), F32),
        compiler_params=_params("parallel"),
        name="merge",
    )(o_mla, o_hg, hg_pack, o_rt, rt_pack, h, w_o, gh, gr)


def _ffn_kernel(h_ref, g_ref, wg_ref, wu_ref, wd_ref, o_ref, xn_sc, acc_sc):
    j = pl.program_id(1)

    @pl.when(j == 0)
    def _():
        x = h_ref[...]
        xn_sc[...] = (_rms(x, x.shape[-1]) * g_ref[...]).astype(BF16)
        acc_sc[...] = jnp.zeros(acc_sc.shape, F32)

    xn = xn_sc[...]
    a = jnp.dot(xn, wg_ref[...], preferred_element_type=F32)
    u = jnp.dot(xn, wu_ref[...], preferred_element_type=F32)
    acc_sc[...] += jnp.dot((a * _sigmoid(a) * u).astype(BF16), wd_ref[...], preferred_element_type=F32)

    @pl.when(j == pl.num_programs(1) - 1)
    def _():
        o_ref[...] = h_ref[...] + acc_sc[...]


def _ffn(h, g, wg, wu, wd):
    n, d = h.shape
    ff = wg.shape[1]
    tm = _row_tile(n, 512)
    tf = ff // 2 if (ff // 2) % LANES == 0 else ff
    return pl.pallas_call(
        _ffn_kernel,
        grid=(n // tm, ff // tf),
        in_specs=[pl.BlockSpec((tm, d), lambda i, j: (i, 0)), pl.BlockSpec(g.shape, lambda i, j: (0, 0)),
                  pl.BlockSpec((d, tf), lambda i, j: (0, j)), pl.BlockSpec((d, tf), lambda i, j: (0, j)),
                  pl.BlockSpec((tf, d), lambda i, j: (j, 0))],
        out_specs=pl.BlockSpec((tm, d), lambda i, j: (i, 0)),
        out_shape=jax.ShapeDtypeStruct((n, d), F32),
        scratch_shapes=[pltpu.VMEM((tm, d), BF16), pltpu.VMEM((tm, d), F32)],
        compiler_params=_params("parallel", "arbitrary"),
        name="ffn",
    )(h, g, wg, wu, wd)


def _moe_kernel(h_ref, g_ref, wr_ref, wg_ref, wu_ref, wd_ref, o_ref, xn_sc, gate_sc, acc_sc, *, ne):
    e = pl.program_id(1)

    @pl.when(e == 0)
    def _():
        x = h_ref[...]
        xn = _rms(x, x.shape[-1]) * g_ref[...]
        xn_sc[...] = xn.astype(BF16)
        acc_sc[...] = jnp.zeros(acc_sc.shape, F32)
        xh, xm, _ = _split3(xn)
        wh, wm, _ = _split3(wr_ref[...])
        dd = lambda a, b: jnp.dot(a, b, preferred_element_type=F32)
        logits = dd(xh, wh) + (dd(xh, wm) + dd(xm, wh))
        lane = lax.broadcasted_iota(jnp.int32, logits.shape, 1)
        valid = lane < ne
        lg = jnp.where(valid, logits, -jnp.inf)
        v1 = jnp.max(lg, axis=-1, keepdims=True)
        i1 = jnp.min(jnp.where(lg == v1, lane, LANES), axis=-1, keepdims=True)
        lg2 = jnp.where(lane == i1, -jnp.inf, lg)
        v2 = jnp.max(lg2, axis=-1, keepdims=True)
        i2 = jnp.min(jnp.where(lg2 == v2, lane, LANES), axis=-1, keepdims=True)
        e2 = jnp.exp(v2 - v1)
        w1 = 1.0 / (1.0 + e2)
        w2 = e2 / (1.0 + e2)
        gate_sc[...] = jnp.where(lane == i1, w1, jnp.where(lane == i2, w2, 0.0))

    lane = lax.broadcasted_iota(jnp.int32, gate_sc.shape, 1)
    gate = jnp.sum(jnp.where(lane == e, gate_sc[...], 0.0), axis=-1, keepdims=True)
    xn = xn_sc[...]
    a = jnp.dot(xn, wg_ref[0], preferred_element_type=F32)
    u = jnp.dot(xn, wu_ref[0], preferred_element_type=F32)
    acc_sc[...] += gate * jnp.dot((a * _sigmoid(a) * u).astype(BF16), wd_ref[0], preferred_element_type=F32)

    @pl.when(e == ne - 1)
    def _():
        o_ref[...] = h_ref[...] + acc_sc[...]


def _moe(h, g, wr, wg, wu, wd):
    n, d = h.shape
    ne, _, ff = wg.shape
    tm = _row_tile(n, 512)
    return pl.pallas_call(
        functools.partial(_moe_kernel, ne=ne),
        grid=(n // tm, ne),
        in_specs=[pl.BlockSpec((tm, d), lambda i, e: (i, 0)), pl.BlockSpec(g.shape, lambda i, e: (0, 0)),
                  pl.BlockSpec(wr.shape, lambda i, e: (0, 0)),
                  pl.BlockSpec((1, d, ff), lambda i, e: (e, 0, 0)), pl.BlockSpec((1, d, ff), lambda i, e: (e, 0, 0)),
                  pl.BlockSpec((1, ff, d), lambda i, e: (e, 0, 0))],
        out_specs=pl.BlockSpec((tm, d), lambda i, e: (i, 0)),
        out_shape=jax.ShapeDtypeStruct((n, d), F32),
        scratch_shapes=[pltpu.VMEM((tm, d), BF16), pltpu.VMEM((tm, LANES), F32), pltpu.VMEM((tm, d), F32)],
        compiler_params=_params("parallel", "arbitrary"),
        name="moe",
    )(h, g, wr, wg, wu, wd)


def _rope_tables(pos, rope_m, ret_dk):
    def angles(d):
        inv = ROPE_BASE ** (-jnp.arange(0, d, 2, dtype=F32) / d)
        return pos.astype(F32)[:, None] * inv[None, :]

    n = pos.shape[0]
    a = angles(rope_m)
    c, s = jnp.cos(a), jnp.sin(a)
    pad = LANES - 64 - rope_m
    cm = jnp.concatenate([jnp.ones((n, 64), F32), c, c, jnp.ones((n, pad), F32)], axis=1)
    sm = jnp.concatenate([jnp.zeros((n, 64), F32), -s, s, jnp.zeros((n, pad), F32)], axis=1)
    a = angles(ret_dk)
    c, s = jnp.cos(a), jnp.sin(a)
    reps = LANES // ret_dk
    cr = jnp.concatenate([c, c] * reps, axis=1)
    sr = jnp.concatenate([-s, s] * reps, axis=1)
    return cm, sm, cr, sr


def _slab_cols(w, nope, rope_m):
    r, nh, dd = w.shape
    return jnp.pad(w, ((0, 0), (0, 0), (0, LANES - dd))).reshape(r, nh * LANES)


def _block_diag_t(s):
    b, nh, dk, dv = s.shape
    eye = jnp.eye(nh, dtype=s.dtype)
    return jnp.einsum('bhkv,hg->bhvgk', s, eye).reshape(b, nh * dv, nh * dk)


def _unblock_t(st, nh):
    b, n, _ = st.shape
    dd = n // nh
    blocks = st.reshape(b, nh, dd, nh, dd)
    return jnp.stack([blocks[:, h, :, h, :] for h in range(nh)], axis=1).swapaxes(-1, -2)


def kernel(x_prompt, x_sample, cache_ckv, cache_krope, page_table, state_hgrn, state_ret, meta_tokens,
           norm_mix_g, w_in, q_lora_g, kv_lora_g, w_uq, w_uk, w_uv, qn_nope_g, qn_rope_g, kn_nope_g, kn_rope_g,
           hg_lb, hg_norm_g, ret_norm_g, w_o, norm_ffn_g, ffn_w_gate, ffn_w_up, ffn_w_down,
           moe_router, moe_w_gate, moe_w_up, moe_w_down):
    depth = w_in.shape[0]
    B, S, D = x_prompt.shape
    Bd, Sd, _ = x_sample.shape
    n_meta = meta_tokens.shape[0]
    nq, nkv = q_lora_g.shape[1], kv_lora_g.shape[1]
    nh, nope, rope_m = w_uq.shape[2], qn_nope_g.shape[1], qn_rope_g.shape[1]
    vdim = w_uv.shape[3]
    hg_heads, hg_dk = state_hgrn.shape[2], state_hgrn.shape[3]
    rt_heads, rt_dk = state_ret.shape[2], state_ret.shape[3]
    dh = hg_heads * hg_dk
    assert nope == 64 and rope_m == 32 and vdim == 64 and hg_dk == 64 and rt_dk == 64
    assert dh == rt_heads * rt_dk and nh % 2 == 0
    past_len = page_table.shape[1] * cache_ckv.shape[2]
    scale = float((nope + rope_m) ** -0.5 * LOG2E)
    ns = Bd * Sd

    h_b = x_prompt.reshape(B * S, D)
    h_s = jnp.concatenate([x_sample.reshape(ns, D), meta_tokens.astype(x_prompt.dtype)], axis=0)
    pos_b = jnp.tile(jnp.arange(S, dtype=F32) + n_meta, B)
    pos_s = jnp.concatenate([jnp.tile(jnp.arange(Sd, dtype=F32) + past_len, Bd), jnp.arange(n_meta, dtype=F32)])
    tabs_b = _rope_tables(pos_b, rope_m, rt_dk)
    tabs_s = _rope_tables(pos_s, rope_m, rt_dk)

    lb_soft = jax.nn.softmax(hg_lb.astype(F32), axis=0)
    lb_all = jnp.cumsum(lb_soft, axis=0) - lb_soft[0]

    lane_pad = lambda g, lo: jnp.pad(g, (lo, LANES - lo - g.shape[0]))[None, :]
    sel = jnp.zeros((LANES, rope_m), BF16).at[jnp.arange(rope_m) + 64, jnp.arange(rope_m)].set(1.0)
    cache_kropet = jnp.swapaxes(cache_krope, 2, 3)

    outs = {k: [] for k in ("ckv_p", "kr_p", "hg_p", "rt_p", "ckv_s", "kr_s", "hg_s", "rt_s")}
    for l in range(depth):
        wi = w_in[l]
        o_kr = nq + nkv
        w_perm = jnp.concatenate(
            [wi[:, :o_kr], jnp.zeros((D, 64), F32), wi[:, o_kr:o_kr + rope_m], jnp.zeros((D, LANES - 64 - rope_m), F32),
             wi[:, o_kr + rope_m:]], axis=1).astype(BF16)
        wq = _slab_cols(w_uq[l], nope, rope_m).astype(BF16)
        wk = _slab_cols(w_uk[l], nope, rope_m).astype(BF16)
        wvt = w_uv[l].reshape(nkv, nh * vdim).T.astype(BF16)
        wukt = w_uk[l].reshape(nkv, nh * nope).T.astype(BF16)
        wabs_w = jnp.einsum('rhd,hg->hdgr', w_uk[l], jnp.eye(nh, dtype=F32))
        wabs_w = jnp.pad(wabs_w, ((0, 0), (0, LANES - nope), (0, 0), (0, 0))).reshape(nh * LANES, nh * nkv).astype(BF16)
        wuv_bd = jnp.einsum('rhd,hg->hrgd', w_uv[l], jnp.eye(nh, dtype=F32)).reshape(nh * nkv, nh * vdim).astype(BF16)
        g_q = (lane_pad(qn_nope_g[l], 0) + lane_pad(qn_rope_g[l], 64)).astype(F32)
        g_k = lane_pad(kn_nope_g[l], 0).astype(F32)
        g_kr = lane_pad(kn_rope_g[l], 64).astype(F32)
        g_hg = jnp.tile(hg_norm_g[l], LANES // hg_dk)[None, :]
        g_rt = jnp.tile(ret_norm_g[l], LANES // rt_dk)[None, :]
        wo = w_o[l].astype(BF16)
        inproj = functools.partial(_inproj, gn=norm_mix_g[l][None, :], w=w_perm, gq=q_lora_g[l][None, :],
                                   gkv=kv_lora_g[l][None, :], gkr=g_kr, lb=lb_all[l][None, :],
                                   nq=nq, nkv=nkv, rope_m=rope_m, dh=dh)
        qprep = functools.partial(_qprep, w=wq, g=g_q, gk=g_k, nh=nh, nope=nope, rope_m=rope_m, scale=scale)

        cq_b, ckv_b, krs_b, hg_b, rt_b = inproj(h_b, tabs=tabs_b)
        cq_s, ckv_s, krs_s, hg_s, rt_s = inproj(h_s, tabs=tabs_s)
        (q_b,) = qprep(cq_b, cm=tabs_b[0], sm=tabs_b[1], with_qg=False)
        q_s, qg_s = qprep(cq_s, cm=tabs_s[0], sm=tabs_s[1], with_qg=True)
        k_b, vt_b = _kvup(ckv_b, krs_b, wk, wvt, g_k, nh=nh, nope=nope)
        k_m, vt_m = _kvup(ckv_s[ns:], krs_s[ns:], wk, wvt, g_k, nh=nh, nope=nope)

        o_b = _flash(q_b, k_b, vt_b, (k_m, vt_m), nh=nh, batch=B)
        mpad = LANES - n_meta
        o_m = _flash(jnp.pad(q_s[ns:], ((0, mpad), (0, 0))), jnp.pad(k_m, ((0, mpad), (0, 0))),
                     jnp.pad(vt_m, ((0, 0), (0, mpad))), None, nh=nh, batch=1)[:n_meta]
        wabs = _matmul(qg_s[:ns], wabs_w, BF16, "qabsorb").reshape(Bd, Sd * nh, nkv)
        pad_new = lambda a: jnp.pad(a.reshape(Bd, Sd, -1), ((0, 0), (0, LANES - Sd), (0, 0)))
        kr_s_compact = krs_s[:, 64:64 + rope_m]
        acc = _paged_attention(page_table, q_s[:ns].reshape(Bd, Sd * nh, LANES), wabs, wukt, sel,
                               pad_new(ckv_s[:ns]), pad_new(kr_s_compact[:ns]).swapaxes(1, 2), cache_ckv, cache_kropet, l,
                               nh=nh, nope=nope, sd=Sd)
        o_s = _matmul(acc.reshape(ns, nh * nkv), wuv_bd, F32, "vabsorb")
        o_small = jnp.concatenate([o_s, o_m], axis=0)

        zero_state = jnp.zeros((1, dh, dh), F32)
        rec = functools.partial(_recurrence, dh=dh, dk=hg_dk)
        ohg_m, sT_hg_m = rec("hgrn", hg_s[ns:][None], zero_state, chunk=SMALL_CHUNK, shared_state=True)
        ohg_s, sT_hg_s = rec("hgrn", hg_s[:ns].reshape(Bd, Sd, -1), _block_diag_t(state_hgrn[l].astype(F32)),
                             chunk=SMALL_CHUNK, shared_state=False)
        ohg_b, sT_hg_b = rec("hgrn", hg_b.reshape(B, S, -1), sT_hg_m, chunk=HG_CHUNK, shared_state=True)
        ort_m, sT_rt_m = rec("ret", rt_s[ns:][None], zero_state, chunk=SMALL_CHUNK, shared_state=True)
        ort_s, sT_rt_s = rec("ret", rt_s[:ns].reshape(Bd, Sd, -1), _block_diag_t(state_ret[l].astype(F32)),
                             chunk=SMALL_CHUNK, shared_state=False)
        ort_b, sT_rt_b = rec("ret", rt_b.reshape(B, S, -1), sT_rt_m, chunk=RET_CHUNK, shared_state=True)
        ohg_small = jnp.concatenate([ohg_s[:, :Sd].reshape(ns, dh), ohg_m[0, :n_meta]], axis=0)
        ort_small = jnp.concatenate([ort_s[:, :Sd].reshape(ns, dh), ort_m[0, :n_meta]], axis=0)

        merge = functools.partial(_merge, w_o=wo, gh=g_hg, gr=g_rt, dh=dh, dv=hg_dk)
        h_b = merge(o_b, ohg_b.reshape(B * S, dh), hg_b, ort_b.reshape(B * S, dh), rt_b, h_b)
        h_s = merge(o_small, ohg_small, hg_s, ort_small, rt_s, h_s)
        g_ffn = norm_ffn_g[l][None, :]
        i = l // 2
        if l % 2 == 0:
            mix = functools.partial(_ffn, g=g_ffn, wg=ffn_w_gate[i].astype(BF16), wu=ffn_w_up[i].astype(BF16),
                                    wd=ffn_w_down[i].astype(BF16))
        else:
            ne = moe_router.shape[2]
            wr = jnp.pad(moe_router[i], ((0, 0), (0, LANES - ne)))
            mix = functools.partial(_moe, g=g_ffn, wr=wr, wg=moe_w_gate[i].astype(BF16), wu=moe_w_up[i].astype(BF16),
                                    wd=moe_w_down[i].astype(BF16))
        h_b = mix(h_b)
        h_s = mix(h_s)

        with_meta = lambda real, meta: jnp.concatenate(
            [jnp.broadcast_to(meta[None], (B,) + meta.shape), real.reshape(B, S, -1)], axis=1)
        outs["ckv_p"].append(with_meta(ckv_b, ckv_s[ns:]))
        outs["kr_p"].append(with_meta(krs_b[:, 64:64 + rope_m], kr_s_compact[ns:]))
        outs["hg_p"].append(_unblock_t(sT_hg_b, hg_heads).astype(state_hgrn.dtype))
        outs["rt_p"].append(_unblock_t(sT_rt_b, rt_heads).astype(state_ret.dtype))
        outs["ckv_s"].append(ckv_s[:ns].reshape(Bd, Sd, nkv))
        outs["kr_s"].append(kr_s_compact[:ns].reshape(Bd, Sd, rope_m))
        outs["hg_s"].append(_unblock_t(sT_hg_s, hg_heads).astype(state_hgrn.dtype))
        outs["rt_s"].append(_unblock_t(sT_rt_s, rt_heads).astype(state_ret.dtype))

    y_prompt = h_b.reshape(B, S, D)
    y_sample = h_s[:ns].reshape(Bd, Sd, D)
    stack = lambda k: jnp.stack(outs[k])
    return (y_prompt, y_sample, stack("ckv_p"), stack("kr_p"), stack("hg_p"), stack("rt_p"),
            stack("ckv_s"), stack("kr_s"), stack("hg_s"), stack("rt_s"))
```

```python
import functools

import numpy as np
import jax
import jax.numpy as jnp
from jax import lax
from jax.experimental import pallas as pl
from jax.experimental.pallas import tpu as pltpu

F32 = jnp.float32
BF16 = jnp.bfloat16

EPS = 1e-6
ROPE_BASE = 10000.0
NEG_BIG = -1e30
LOG2E = 1.4426950408889634
LANES = 128
SUBLANES = 8
VMEM_LIMIT = 56 * 1024 * 1024

HG_CHUNK = 64
RET_CHUNK = 128
SMALL_CHUNK = 16
REC_BATCH_BLOCK = 8
PAGES_PER_STEP = 16
SCORE_BUFFERS = 3
FLASH_TILE = 1024


def _params(*sem):
    return pltpu.CompilerParams(dimension_semantics=sem, vmem_limit_bytes=VMEM_LIMIT)


def _row_tile(n, pref):
    if n <= pref:
        return n
    t = pref - pref % 16
    while t >= 16:
        if n % t == 0:
            return t
        t -= 16
    return n


def _dot(a, b):
    return jnp.dot(a.astype(BF16), b.astype(BF16), preferred_element_type=F32)


def _dot_nt(a, b):
    return lax.dot_general(a.astype(BF16), b.astype(BF16), (((1,), (1,)), ((), ())), preferred_element_type=F32)


def _dot_tn(a, b):
    return lax.dot_general(a.astype(BF16), b.astype(BF16), (((0,), (0,)), ((), ())), preferred_element_type=F32)


def _rms(x, width):
    return x * lax.rsqrt(jnp.sum(x * x, axis=-1, keepdims=True) * (1.0 / width) + EPS)


def _rope128(y, cos, sin_signed, half):
    lane = lax.broadcasted_iota(jnp.int32, y.shape, 1)
    first = (lane % (2 * half)) < half
    rot = jnp.where(first, pltpu.roll(y, LANES - half, 1), pltpu.roll(y, half, 1))
    return y * cos + rot * sin_signed


def _sigmoid(x):
    return 1.0 / (1.0 + jnp.exp(-x))


def _inproj_kernel(h_ref, gn_ref, w_ref, gq_ref, gkv_ref, gkr_ref, lb_ref, cm_ref, sm_ref, cr_ref, sr_ref,
                   cq_ref, ckv_ref, kr_ref, hg_ref, rt_ref, *, nq, nkv, rope_m, dh):
    x = h_ref[...]
    xb = (_rms(x, x.shape[-1]) * gn_ref[...]).astype(BF16)

    def proj(lo, width):
        return jnp.dot(xb, w_ref[:, lo:lo + width], preferred_element_type=F32)

    c = proj(0, nq)
    cq_ref[...] = (_rms(c, nq) * gq_ref[...]).astype(BF16)
    c = proj(nq, nkv)
    ckv_ref[...] = _rms(c, nkv) * gkv_ref[...]
    k = proj(nq + nkv, LANES)
    k = _rms(k, rope_m) * gkr_ref[...]
    kr_ref[...] = _rope128(k, cm_ref[...], sm_ref[...], rope_m // 2)

    base = nq + nkv + LANES
    hq = proj(base, dh)
    hg_ref[:, 0:dh] = hq * _sigmoid(hq)
    z = proj(base + dh, dh)
    lb = lb_ref[...]
    f = lb + (1.0 - lb) * _sigmoid(z)
    hg_ref[:, dh:2 * dh] = 1.0 - f
    hg_ref[:, 2 * dh:3 * dh] = jnp.log(f)
    hg_ref[:, 3 * dh:4 * dh] = proj(base + 2 * dh, dh)
    hg_ref[:, 4 * dh:5 * dh] = proj(base + 3 * dh, dh)

    base = base + 4 * dh
    cr = cr_ref[...]
    sr = sr_ref[...]
    for part, scale in ((0, 1.0), (1, 64 ** -0.5)):
        for s in range(dh // LANES):
            lo = part * dh + s * LANES
            y = _rope128(proj(base + lo, LANES), cr, sr, 32)
            rt_ref[:, lo:lo + LANES] = y * scale
    rt_ref[:, 2 * dh:3 * dh] = proj(base + 2 * dh, dh)
    rt_ref[:, 3 * dh:4 * dh] = proj(base + 3 * dh, dh)


def _inproj(h, gn, w, gq, gkv, gkr, lb, tabs, *, nq, nkv, rope_m, dh):
    n, d = h.shape
    tm = _row_tile(n, 512)
    cm, sm, cr, sr = tabs
    row = lambda width: pl.BlockSpec((tm, width), lambda i: (i, 0))
    const = lambda a: pl.BlockSpec(a.shape, lambda i: (0, 0))
    return pl.pallas_call(
        functools.partial(_inproj_kernel, nq=nq, nkv=nkv, rope_m=rope_m, dh=dh),
        grid=(n // tm,),
        in_specs=[row(d), const(gn), const(w), const(gq), const(gkv), const(gkr), const(lb),
                  row(LANES), row(LANES), row(LANES), row(LANES)],
        out_specs=[row(nq), row(nkv), row(LANES), row(5 * dh), row(4 * dh)],
        out_shape=[jax.ShapeDtypeStruct((n, nq), BF16), jax.ShapeDtypeStruct((n, nkv), F32),
                   jax.ShapeDtypeStruct((n, LANES), F32), jax.ShapeDtypeStruct((n, 5 * dh), F32),
                   jax.ShapeDtypeStruct((n, 4 * dh), F32)],
        compiler_params=_params("parallel"),
        name="inproj",
    )(h, gn, w, gq, gkv, gkr, lb, cm, sm, cr, sr)


def _qprep_kernel(c_ref, w_ref, g_ref, gk_ref, cm_ref, sm_ref, *out_refs, nh, nope, rope_m, scale, with_qg):
    c = c_ref[...]
    cm = cm_ref[...]
    sm = sm_ref[...]
    g = g_ref[...]
    for h in range(nh):
        x = jnp.dot(c, w_ref[:, h * LANES:(h + 1) * LANES], preferred_element_type=F32)
        lane = lax.broadcasted_iota(jnp.int32, x.shape, 1)
        is_nope = lane < nope
        x2 = x * x
        ss_n = jnp.sum(jnp.where(is_nope, x2, 0.0), axis=-1, keepdims=True)
        ss_r = jnp.sum(jnp.where(is_nope, 0.0, x2), axis=-1, keepdims=True)
        inv = jnp.where(is_nope, lax.rsqrt(ss_n * (1.0 / nope) + EPS), lax.rsqrt(ss_r * (1.0 / rope_m) + EPS))
        y = _rope128(x * inv * g, cm, sm, rope_m // 2) * scale
        out_refs[0][:, h * LANES:(h + 1) * LANES] = y.astype(BF16)
        if with_qg:
            out_refs[1][:, h * LANES:(h + 1) * LANES] = (y * gk_ref[...]).astype(BF16)


def _qprep(cqn, w, g, gk, cm, sm, *, nh, nope, rope_m, scale, with_qg):
    n, r = cqn.shape
    tm = _row_tile(n, 512)
    row = lambda width: pl.BlockSpec((tm, width), lambda i: (i, 0))
    const = lambda a: pl.BlockSpec(a.shape, lambda i: (0, 0))
    nout = 2 if with_qg else 1
    return pl.pallas_call(
        functools.partial(_qprep_kernel, nh=nh, nope=nope, rope_m=rope_m, scale=scale, with_qg=with_qg),
        grid=(n // tm,),
        in_specs=[row(r), const(w), const(g), const(gk), row(LANES), row(LANES)],
        out_specs=[row(nh * LANES)] * nout,
        out_shape=[jax.ShapeDtypeStruct((n, nh * LANES), BF16)] * nout,
        compiler_params=_params("parallel"),
        name="qprep",
    )(cqn, w, g, gk, cm, sm)


def _kvup_kernel(c_ref, kr_ref, wk_ref, wvt_ref, gk_ref, k_ref, vt_ref, *, nh, nope):
    c = c_ref[...].astype(BF16)
    kr = kr_ref[...]
    gk = gk_ref[...]
    for h in range(nh):
        x = jnp.dot(c, wk_ref[:, h * LANES:(h + 1) * LANES], preferred_element_type=F32)
        k_ref[:, h * LANES:(h + 1) * LANES] = (_rms(x, nope) * gk + kr).astype(BF16)
    vt_ref[...] = _dot_nt(wvt_ref[...], c).astype(BF16)


def _kvup(ckv, krs, wk, wvt, gk, *, nh, nope):
    n, r = ckv.shape
    tm = _row_tile(n, 512)
    row = lambda width: pl.BlockSpec((tm, width), lambda i: (i, 0))
    const = lambda a: pl.BlockSpec(a.shape, lambda i: (0, 0))
    nv = wvt.shape[0]
    return pl.pallas_call(
        functools.partial(_kvup_kernel, nh=nh, nope=nope),
        grid=(n // tm,),
        in_specs=[row(r), row(LANES), const(wk), const(wvt), const(gk)],
        out_specs=[row(nh * LANES), pl.BlockSpec((nv, tm), lambda i: (0, i))],
        out_shape=[jax.ShapeDtypeStruct((n, nh * LANES), BF16), jax.ShapeDtypeStruct((nv, n), BF16)],
        compiler_params=_params("parallel"),
        name="kvup",
    )(ckv, krs, wk, wvt, gk)


def _flash_kernel(*refs, nh, vdim, has_prefix):
    if has_prefix:
        q_ref, k_ref, vt_ref, kp_ref, vtp_ref, o_ref, m_sc, l_sc, acc_sc, s_sc = refs
    else:
        q_ref, k_ref, vt_ref, o_ref, m_sc, l_sc, acc_sc, s_sc = refs
    qi = pl.program_id(1)
    ki = pl.program_id(2)

    def update(k_r, vt_r, masked):
        tk = k_r.shape[0]

        def scores(h):
            st = _dot_nt(k_r[:, h * LANES:(h + 1) * LANES], q_ref[:, h * LANES:(h + 1) * LANES])
            if masked:
                key = lax.broadcasted_iota(jnp.int32, st.shape, 0)
                qry = lax.broadcasted_iota(jnp.int32, st.shape, 1)
                st = jnp.where(key <= qry, st, NEG_BIG)
            s_sc[h % nbuf, 0:tk, :] = st

        nbuf = s_sc.shape[0]
        for h in range(nbuf - 1):
            scores(h)
        for h in range(nh):
            if h + nbuf - 1 < nh:
                scores(h + nbuf - 1)
            rows = slice(h * vdim, (h + 1) * vdim)
            st = s_sc[h % nbuf, 0:tk, :]
            m_prev = m_sc[h:h + 1, :]
            m_new = jnp.maximum(m_prev, jnp.max(st, axis=0, keepdims=True))
            alpha = jnp.exp2(m_prev - m_new)
            e = jnp.exp2(st - m_new)
            l_sc[h:h + 1, :] = alpha * l_sc[h:h + 1, :] + jnp.sum(e, axis=0, keepdims=True)
            m_sc[h:h + 1, :] = m_new
            pv = jnp.dot(vt_r[rows, :], e.astype(BF16), preferred_element_type=F32)
            acc_sc[rows, :] = acc_sc[rows, :] * alpha + pv

    @pl.when(ki == 0)
    def _():
        m_sc[...] = jnp.full(m_sc.shape, NEG_BIG, F32)
        l_sc[...] = jnp.zeros(l_sc.shape, F32)
        acc_sc[...] = jnp.zeros(acc_sc.shape, F32)
        if has_prefix:
            update(kp_ref, vtp_ref, False)

    @pl.when(ki < qi)
    def _():
        update(k_ref, vt_ref, False)

    @pl.when(ki == qi)
    def _():
        update(k_ref, vt_ref, True)
        for h in range(nh):
            rows = slice(h * vdim, (h + 1) * vdim)
            acc_sc[rows, :] = acc_sc[rows, :] / l_sc[h:h + 1, :]
        o_ref[...] = acc_sc[...].T


def _flash(q, k, vt, prefix, *, nh, batch):
    n = q.shape[0]
    t = n // batch
    dv = vt.shape[0]
    tq = _row_tile(t, FLASH_TILE)
    nt = t // tq
    qspec = pl.BlockSpec((tq, nh * LANES), lambda bi, qi, ki: (bi * nt + qi, 0))
    kspec = pl.BlockSpec((tq, nh * LANES), lambda bi, qi, ki: (bi * nt + jnp.minimum(ki, qi), 0))
    vspec = pl.BlockSpec((dv, tq), lambda bi, qi, ki: (0, bi * nt + jnp.minimum(ki, qi)))
    ins, specs = [q, k, vt], [qspec, kspec, vspec]
    if prefix is not None:
        for a in prefix:
            ins.append(a)
            specs.append(pl.BlockSpec(a.shape, lambda bi, qi, ki: (0, 0)))
    return pl.pallas_call(
        functools.partial(_flash_kernel, nh=nh, vdim=dv // nh, has_prefix=prefix is not None),
        grid=(batch, nt, nt),
        in_specs=specs,
        out_specs=pl.BlockSpec((tq, dv), lambda bi, qi, ki: (bi * nt + qi, 0)),
        out_shape=jax.ShapeDtypeStruct((n, dv), F32),
        scratch_shapes=[pltpu.VMEM((nh, tq), F32), pltpu.VMEM((nh, tq), F32), pltpu.VMEM((dv, tq), F32),
                        pltpu.VMEM((SCORE_BUFFERS, tq, tq), F32)],
        compiler_params=_params("parallel", "parallel", "arbitrary"),
        name="flash_prefix" if prefix is not None else "flash",
    )(*ins)


def _matmul_kernel(a_ref, w_ref, o_ref):
    o_ref[...] = jnp.dot(a_ref[...].astype(BF16), w_ref[...], preferred_element_type=F32).astype(o_ref.dtype)


def _matmul(a, w, out_dtype, name):
    n, k = a.shape
    tm = _row_tile(n, 512)
    return pl.pallas_call(
        _matmul_kernel,
        grid=(n // tm,),
        in_specs=[pl.BlockSpec((tm, k), lambda i: (i, 0)), pl.BlockSpec(w.shape, lambda i: (0, 0))],
        out_specs=pl.BlockSpec((tm, w.shape[1]), lambda i: (i, 0)),
        out_shape=jax.ShapeDtypeStruct((n, w.shape[1]), out_dtype),
        compiler_params=_params("parallel"),
        name=name,
    )(a, w)


def _paged_kernel(pt_ref, q_ref, wabs_ref, wukt_ref, sel_ref, cn_ref, krn_ref, ckv_hbm, krt_hbm, o_ref,
                  lhs_sc, cbuf, kbuf, sem, *, layer, npg, nchunk, sub, nh, nope, sd):
    b = pl.program_id(0)
    nb = pl.num_programs(0)
    nk = nh * nope
    rows = sd * nh
    page = ckv_hbm.shape[2]
    nslot = cbuf.shape[0]

    def chunk_slot(off):
        return lax.rem(b * nchunk + off, nslot)

    def page_copies(off, p):
        slot = chunk_slot(off)
        idx = pt_ref[b + off // nchunk, (off % nchunk) * npg + p]
        span = pl.ds(p * page, page)
        return (pltpu.make_async_copy(ckv_hbm.at[layer, idx], cbuf.at[slot, span, :], sem.at[0, slot]),
                pltpu.make_async_copy(krt_hbm.at[layer, idx], kbuf.at[slot, :, span], sem.at[1, slot]))

    def start_chunk(off):
        for p in range(npg):
            for cp in page_copies(off, p):
                cp.start()

    def wait_chunk(off):
        for p in range(npg):
            for cp in page_copies(off, p):
                cp.wait()

    @pl.when(b == 0)
    def _():
        start_chunk(0)
        start_chunk(1)

    lhs_sc[0:nk, :] = wukt_ref[...]
    lhs_sc[nk:nk + rows, :] = wabs_ref[0]
    qr = jnp.dot(q_ref[0], sel_ref[...], preferred_element_type=F32).astype(BF16)

    def scores(c, krt):
        big = _dot_nt(lhs_sc[...], c)
        kraw = big[0:nk]
        ssq = jnp.sum((kraw * kraw).reshape(nh, nope, kraw.shape[-1]), axis=1)
        r = lax.rsqrt(ssq * (1.0 / nope) + EPS)
        return big[nk:nk + rows] * jnp.concatenate([r] * sd, axis=0) + jnp.dot(qr, krt, preferred_element_type=F32)

    def update(state, s, c):
        m_prev, l_prev, acc = state
        m_new = jnp.maximum(m_prev, jnp.max(s, axis=-1, keepdims=True))
        alpha = jnp.exp2(m_prev - m_new)
        e = jnp.exp2(s - m_new)
        l_new = alpha * l_prev + jnp.sum(e, axis=-1, keepdims=True)
        return m_new, l_new, acc * alpha + jnp.dot(e.astype(BF16), c, preferred_element_type=F32)

    def chunk_scores(off):
        slot = chunk_slot(off)
        span = sub * page
        cs, ss = [], []
        for g in range(npg // sub):
            lat = cbuf[slot, g * span:(g + 1) * span, :].astype(BF16)
            cs.append(lat)
            ss.append(scores(lat, kbuf[slot, :, g * span:(g + 1) * span].astype(BF16)))
        return jnp.concatenate(ss, axis=1), jnp.concatenate(cs, axis=0)

    state = (jnp.full((rows, 1), NEG_BIG, F32), jnp.zeros((rows, 1), F32), jnp.zeros((rows, cbuf.shape[2]), F32))
    wait_chunk(0)
    cur = chunk_scores(0)
    for c in range(nchunk):
        if c + 2 < nchunk:
            start_chunk(c + 2)
        else:
            @pl.when(b + 1 < nb)
            def _():
                start_chunk(c + 2)
        if c + 1 < nchunk:
            wait_chunk(c + 1)
            nxt = chunk_scores(c + 1)
        state = update(state, *cur)
        if c + 1 < nchunk:
            cur = nxt

    lat = cn_ref[0].astype(BF16)
    s = scores(lat, krn_ref[0].astype(BF16))
    row = lax.broadcasted_iota(jnp.int32, s.shape, 0)
    col = lax.broadcasted_iota(jnp.int32, s.shape, 1)
    _, l_fin, acc = update(state, jnp.where(col * nh <= row, s, NEG_BIG), lat)
    o_ref[0] = acc / l_fin


def _paged_attention(page_table, q, wabs, wukt, sel, cn, krnt, cache_ckv, cache_kropet, layer, *, nh, nope, sd):
    bd, rows, _ = q.shape
    n_pages = page_table.shape[1]
    assert n_pages % 2 == 0
    npg = max(p for p in range(1, PAGES_PER_STEP + 1) if n_pages % (2 * p) == 0)
    nchunk = n_pages // npg
    nslot = 3
    sub = 2 if npg % 2 == 0 else 1
    page, kvl = cache_ckv.shape[2:]
    rp = cache_kropet.shape[2]
    per_b = lambda a: pl.BlockSpec((1,) + a.shape[1:], lambda b, pt: (b, 0, 0))
    const = lambda a: pl.BlockSpec(a.shape, lambda b, pt: (0, 0))
    hbm = pl.BlockSpec(memory_space=pl.ANY)
    grid_spec = pltpu.PrefetchScalarGridSpec(
        num_scalar_prefetch=1,
        grid=(bd,),
        in_specs=[per_b(q), per_b(wabs), const(wukt), const(sel), per_b(cn), per_b(krnt), hbm, hbm],
        out_specs=pl.BlockSpec((1, rows, kvl), lambda b, pt: (b, 0, 0)),
        scratch_shapes=[pltpu.VMEM((nh * nope + rows, kvl), BF16), pltpu.VMEM((nslot, npg * page, kvl), F32),
                        pltpu.VMEM((nslot, rp, npg * page), F32), pltpu.SemaphoreType.DMA((2, nslot))],
    )
    return pl.pallas_call(
        functools.partial(_paged_kernel, layer=layer, npg=npg, nchunk=nchunk, sub=sub, nh=nh, nope=nope, sd=sd),
        grid_spec=grid_spec,
        out_shape=jax.ShapeDtypeStruct((bd, rows, kvl), F32),
        compiler_params=_params("arbitrary"),
        name="paged_attention",
    )(page_table, q, wabs, wukt, sel, cn, krnt, cache_ckv, cache_kropet)


def _block_mask(n, blk):
    r = lax.broadcasted_iota(jnp.int32, (n, n), 0) // blk
    c = lax.broadcasted_iota(jnp.int32, (n, n), 1) // blk
    return r == c


def _split3(x):
    hi = x.astype(BF16)
    r1 = x - hi.astype(F32)
    mid = r1.astype(BF16)
    lo = (r1 - mid.astype(F32)).astype(BF16)
    return hi, mid, lo


def _hgrn_kernel(x_ref, s0_ref, o_ref, sT_ref, st_sc, pad_sc, sh_sc, e_sc, *, dh, dk, chunk, bb):
    for i in range(bb):
        _hgrn_one(x_ref.at[i], s0_ref.at[i % s0_ref.shape[0]], o_ref.at[i], sT_ref.at[i],
                  st_sc.at[i], pad_sc.at[i], sh_sc.at[i], e_sc.at[i], dh=dh, dk=dk, chunk=chunk)


def _hgrn_one(x_ref, s0_ref, o_ref, sT_ref, st_sc, pad_sc, sh_sc, e_sc, *, dh, dk, chunk):
    ci = pl.program_id(1)

    @pl.when(ci == 0)
    def _():
        st_sc[...] = s0_ref[...]
        pad_sc[:, 0:SUBLANES, :] = jnp.zeros((3, SUBLANES, dh), F32)

    q = x_ref[:, 0:dh]
    k = x_ref[:, dh:2 * dh]
    g = x_ref[:, 2 * dh:3 * dh]
    v = x_ref[:, 3 * dh:4 * dh]
    r = lax.broadcasted_iota(jnp.int32, (chunk, chunk), 0)
    c = lax.broadcasted_iota(jnp.int32, (chunk, chunk), 1)
    tri = jnp.where(c <= r, 1.0, 0.0).astype(BF16)
    b = sum(jnp.dot(tri, part, preferred_element_type=F32) for part in _split3(g * LOG2E))
    for i, val in enumerate((k, b, v)):
        pad_sc[i, SUBLANES:SUBLANES + chunk, :] = val
        sh_sc[i, 0] = val
        for r in range(1, SUBLANES):
            sh_sc[i, r] = pad_sc[i, SUBLANES - r:SUBLANES - r + chunk, :]
    ones_bd = jnp.where(_block_mask(dh, dk), 1.0, 0.0).astype(BF16)

    o_ref[...] = _dot_nt(q * jnp.exp2(b), st_sc[...])
    bands = []
    off = 0
    for a in range(chunk // SUBLANES):
        lo = a * SUBLANES
        rows = chunk - lo
        qa = q[lo:, :]
        ba = b[lo:, :]
        for r in range(SUBLANES):
            e_sc[off:off + rows, :] = qa * sh_sc[0, r, 0:rows, :] * jnp.exp2(ba - sh_sc[1, r, 0:rows, :])
            bands.append((lo, rows, r, off))
            off += rows
    e_sc[...] = jnp.dot(e_sc[...].astype(BF16), ones_bd, preferred_element_type=F32)
    for a in range(chunk // SUBLANES):
        acc = None
        for lo, rows, r, off in bands[a * SUBLANES:(a + 1) * SUBLANES]:
            term = e_sc[off:off + rows, :] * sh_sc[2, r, 0:rows, :]
            acc = term if acc is None else acc + term
        o_ref[lo:, :] += acc

    b_last = b[chunk - 1:chunk, :]
    upd = _dot_tn(v, k * jnp.exp2(b_last - b))
    st_sc[...] = st_sc[...] * jnp.exp2(b_last) + jnp.where(_block_mask(dh, dk), upd, 0.0)

    @pl.when(ci == pl.num_programs(1) - 1)
    def _():
        sT_ref[...] = st_sc[...]


def _ret_kernel(x_ref, s0_ref, o_ref, sT_ref, st_sc, *, bb, **kw):
    for i in range(bb):
        _ret_one(x_ref.at[i], s0_ref.at[i % s0_ref.shape[0]], o_ref.at[i], sT_ref.at[i], st_sc.at[i], **kw)


def _ret_one(x_ref, s0_ref, o_ref, sT_ref, st_sc, *, dh, dk, chunk, n_valid, log_gamma):
    ci = pl.program_id(1)

    @pl.when(ci == 0)
    def _():
        st_sc[...] = s0_ref[...]

    q = x_ref[:, 0:dh]
    k = x_ref[:, dh:2 * dh]
    v = x_ref[:, 2 * dh:3 * dh]
    lane = lax.broadcasted_iota(jnp.int32, (chunk, dh), 1)
    head = lane // dk
    lg = jnp.zeros((chunk, dh), F32)
    for h, val in enumerate(log_gamma):
        lg = jnp.where(head == h, val, lg)
    t = lax.broadcasted_iota(jnp.int32, (chunk, dh), 0).astype(F32)
    r = lax.broadcasted_iota(jnp.int32, (chunk, chunk), 0)
    c = lax.broadcasted_iota(jnp.int32, (chunk, chunk), 1)
    diff = (r - c).astype(F32)

    o = _dot_nt(q, st_sc[...]) * jnp.exp((t + 1.0) * lg)
    for h, val in enumerate(log_gamma):
        in_h = head == h
        a = _dot_nt(jnp.where(in_h, q, 0.0), k)
        a = a * jnp.where(diff >= 0, jnp.exp(jnp.maximum(diff, 0.0) * val), 0.0)
        o = o + jnp.where(in_h, _dot(a, v), 0.0)
    o_ref[...] = o

    upd = _dot_tn(v, k * jnp.exp((n_valid - 1.0 - t) * lg))
    st_sc[...] = st_sc[...] * jnp.exp(n_valid * lg[0:1, :]) + jnp.where(_block_mask(dh, dk), upd, 0.0)

    @pl.when(ci == pl.num_programs(1) - 1)
    def _():
        sT_ref[...] = st_sc[...]


def _recurrence(kind, x, s0, *, dh, dk, chunk, shared_state):
    b, t, w = x.shape
    n_valid = min(chunk, t)
    if t < chunk:
        x = jnp.pad(x, ((0, 0), (0, chunk - t), (0, 0)))
    t_pad = x.shape[1]
    bb = max(d for d in range(1, REC_BATCH_BLOCK + 1) if b % d == 0)
    s0_spec = (pl.BlockSpec((1, dh, dh), lambda bi, ci: (0, 0, 0)) if shared_state
               else pl.BlockSpec((bb, dh, dh), lambda bi, ci: (bi, 0, 0)))
    if kind == "hgrn":
        body = functools.partial(_hgrn_kernel, dh=dh, dk=dk, chunk=chunk, bb=bb)
        band_rows = SUBLANES * sum(range(SUBLANES, chunk + 1, SUBLANES))
        scratch = [pltpu.VMEM((bb, dh, dh), F32), pltpu.VMEM((bb, 3, SUBLANES + chunk, dh), F32),
                   pltpu.VMEM((bb, 3, SUBLANES, chunk, dh), F32), pltpu.VMEM((bb, band_rows, dh), F32)]
    else:
        log_gamma = tuple(float(np.log1p(-np.exp2(-5.0 - h))) for h in range(dh // dk))
        body = functools.partial(_ret_kernel, bb=bb, dh=dh, dk=dk, chunk=chunk, n_valid=n_valid,
                                 log_gamma=log_gamma)
        scratch = [pltpu.VMEM((bb, dh, dh), F32)]
    return pl.pallas_call(
        body,
        grid=(b // bb, t_pad // chunk),
        in_specs=[pl.BlockSpec((bb, chunk, w), lambda bi, ci: (bi, ci, 0)), s0_spec],
        out_specs=[pl.BlockSpec((bb, chunk, dh), lambda bi, ci: (bi, ci, 0)),
                   pl.BlockSpec((bb, dh, dh), lambda bi, ci: (bi, 0, 0))],
        out_shape=[jax.ShapeDtypeStruct((b, t_pad, dh), F32), jax.ShapeDtypeStruct((b, dh, dh), F32)],
        scratch_shapes=scratch,
        compiler_params=_params("parallel", "arbitrary"),
        name=kind,
    )(x, s0)


def _merge_kernel(om_ref, oh_ref, hg_ref, or_ref, rg_ref, h_ref, w_ref, gh_ref, gr_ref, o_ref, *, nm, dh, dv):
    def gated(o_r, gate_r, g_r):
        o = o_r[...]
        gate = gate_r[...]
        out = []
        for s in range(dh // LANES):
            x = o[:, s * LANES:(s + 1) * LANES]
            lane = lax.broadcasted_iota(jnp.int32, x.shape, 1)
            low = lane < dv
            x2 = x * x
            ss_l = jnp.sum(jnp.where(low, x2, 0.0), axis=-1, keepdims=True)
            ss_h = jnp.sum(jnp.where(low, 0.0, x2), axis=-1, keepdims=True)
            inv = lax.rsqrt(jnp.where(low, ss_l, ss_h) * (1.0 / dv) + EPS)
            gt = gate[:, s * LANES:(s + 1) * LANES]
            out.append(x * inv * g_r[...] * (gt * _sigmoid(gt)))
        return jnp.concatenate(out, axis=-1)

    y = _dot(om_ref[...], w_ref[0:nm, :])
    y = y + _dot(gated(oh_ref, hg_ref, gh_ref), w_ref[nm:nm + dh, :])
    y = y + _dot(gated(or_ref, rg_ref, gr_ref), w_ref[nm + dh:nm + 2 * dh, :])
    o_ref[...] = h_ref[...] + y


def _merge(o_mla, o_hg, hg_pack, o_rt, rt_pack, h, w_o, gh, gr, *, dh, dv):
    n, d = h.shape
    nm = o_mla.shape[1]
    tm = _row_tile(n, 512)
    row = lambda width, cb=0: pl.BlockSpec((tm, width), lambda i: (i, cb))
    const = lambda a: pl.BlockSpec(a.shape, lambda i: (0, 0))
    return pl.pallas_call(
        functools.partial(_merge_kernel, nm=nm, dh=dh, dv=dv),
        grid=(n // tm,),
        in_specs=[row(nm), row(dh), row(dh, 4), row(dh), row(dh, 3), row(d), const(w_o), const(gh), const(gr)],
        out_specs=row(d),
        out_shape=jax.ShapeDtypeStruct((n, d), F32),
        compiler_params=_params("parallel"),
        name="merge",
    )(o_mla, o_hg, hg_pack, o_rt, rt_pack, h, w_o, gh, gr)


def _ffn_kernel(h_ref, g_ref, wg_ref, wu_ref, wd_ref, o_ref, xn_sc, acc_sc):
    j = pl.program_id(1)

    @pl.when(j == 0)
    def _():
        x = h_ref[...]
        xn_sc[...] = (_rms(x, x.shape[-1]) * g_ref[...]).astype(BF16)
        acc_sc[...] = jnp.zeros(acc_sc.shape, F32)

    xn = xn_sc[...]
    a = jnp.dot(xn, wg_ref[...], preferred_element_type=F32)
    u = jnp.dot(xn, wu_ref[...], preferred_element_type=F32)
    acc_sc[...] += jnp.dot((a * _sigmoid(a) * u).astype(BF16), wd_ref[...], preferred_element_type=F32)

    @pl.when(j == pl.num_programs(1) - 1)
    def _():
        o_ref[...] = h_ref[...] + acc_sc[...]


def _ffn(h, g, wg, wu, wd):
    n, d = h.shape
    ff = wg.shape[1]
    tm = _row_tile(n, 512)
    tf = ff // 2 if (ff // 2) % LANES == 0 else ff
    return pl.pallas_call(
        _ffn_kernel,
        grid=(n // tm, ff // tf),
        in_specs=[pl.BlockSpec((tm, d), lambda i, j: (i, 0)), pl.BlockSpec(g.shape, lambda i, j: (0, 0)),
                  pl.BlockSpec((d, tf), lambda i, j: (0, j)), pl.BlockSpec((d, tf), lambda i, j: (0, j)),
                  pl.BlockSpec((tf, d), lambda i, j: (j, 0))],
        out_specs=pl.BlockSpec((tm, d), lambda i, j: (i, 0)),
        out_shape=jax.ShapeDtypeStruct((n, d), F32),
        scratch_shapes=[pltpu.VMEM((tm, d), BF16), pltpu.VMEM((tm, d), F32)],
        compiler_params=_params("parallel", "arbitrary"),
        name="ffn",
    )(h, g, wg, wu, wd)


def _moe_kernel(h_ref, g_ref, wr_ref, wg_ref, wu_ref, wd_ref, o_ref, xn_sc, gate_sc, acc_sc, *, ne):
    e = pl.program_id(1)

    @pl.when(e == 0)
    def _():
        x = h_ref[...]
        xn = _rms(x, x.shape[-1]) * g_ref[...]
        xn_sc[...] = xn.astype(BF16)
        acc_sc[...] = jnp.zeros(acc_sc.shape, F32)
        xh, xm, _ = _split3(xn)
        wh, wm, _ = _split3(wr_ref[...])
        dd = lambda a, b: jnp.dot(a, b, preferred_element_type=F32)
        logits = dd(xh, wh) + (dd(xh, wm) + dd(xm, wh))
        lane = lax.broadcasted_iota(jnp.int32, logits.shape, 1)
        valid = lane < ne
        lg = jnp.where(valid, logits, -jnp.inf)
        v1 = jnp.max(lg, axis=-1, keepdims=True)
        i1 = jnp.min(jnp.where(lg == v1, lane, LANES), axis=-1, keepdims=True)
        lg2 = jnp.where(lane == i1, -jnp.inf, lg)
        v2 = jnp.max(lg2, axis=-1, keepdims=True)
        i2 = jnp.min(jnp.where(lg2 == v2, lane, LANES), axis=-1, keepdims=True)
        e2 = jnp.exp(v2 - v1)
        w1 = 1.0 / (1.0 + e2)
        w2 = e2 / (1.0 + e2)
        gate_sc[...] = jnp.where(lane == i1, w1, jnp.where(lane == i2, w2, 0.0))

    lane = lax.broadcasted_iota(jnp.int32, gate_sc.shape, 1)
    gate = jnp.sum(jnp.where(lane == e, gate_sc[...], 0.0), axis=-1, keepdims=True)
    xn = xn_sc[...]
    a = jnp.dot(xn, wg_ref[0], preferred_element_type=F32)
    u = jnp.dot(xn, wu_ref[0], preferred_element_type=F32)
    acc_sc[...] += gate * jnp.dot((a * _sigmoid(a) * u).astype(BF16), wd_ref[0], preferred_element_type=F32)

    @pl.when(e == ne - 1)
    def _():
        o_ref[...] = h_ref[...] + acc_sc[...]


def _moe(h, g, wr, wg, wu, wd):
    n, d = h.shape
    ne, _, ff = wg.shape
    tm = _row_tile(n, 512)
    return pl.pallas_call(
        functools.partial(_moe_kernel, ne=ne),
        grid=(n // tm, ne),
        in_specs=[pl.BlockSpec((tm, d), lambda i, e: (i, 0)), pl.BlockSpec(g.shape, lambda i, e: (0, 0)),
                  pl.BlockSpec(wr.shape, lambda i, e: (0, 0)),
                  pl.BlockSpec((1, d, ff), lambda i, e: (e, 0, 0)), pl.BlockSpec((1, d, ff), lambda i, e: (e, 0, 0)),
                  pl.BlockSpec((1, ff, d), lambda i, e: (e, 0, 0))],
        out_specs=pl.BlockSpec((tm, d), lambda i, e: (i, 0)),
        out_shape=jax.ShapeDtypeStruct((n, d), F32),
        scratch_shapes=[pltpu.VMEM((tm, d), BF16), pltpu.VMEM((tm, LANES), F32), pltpu.VMEM((tm, d), F32)],
        compiler_params=_params("parallel", "arbitrary"),
        name="moe",
    )(h, g, wr, wg, wu, wd)


def _rope_tables(pos, rope_m, ret_dk):
    def angles(d):
        inv = ROPE_BASE ** (-jnp.arange(0, d, 2, dtype=F32) / d)
        return pos.astype(F32)[:, None] * inv[None, :]

    n = pos.shape[0]
    a = angles(rope_m)
    c, s = jnp.cos(a), jnp.sin(a)
    pad = LANES - 64 - rope_m
    cm = jnp.concatenate([jnp.ones((n, 64), F32), c, c, jnp.ones((n, pad), F32)], axis=1)
    sm = jnp.concatenate([jnp.zeros((n, 64), F32), -s, s, jnp.zeros((n, pad), F32)], axis=1)
    a = angles(ret_dk)
    c, s = jnp.cos(a), jnp.sin(a)
    reps = LANES // ret_dk
    cr = jnp.concatenate([c, c] * reps, axis=1)
    sr = jnp.concatenate([-s, s] * reps, axis=1)
    return cm, sm, cr, sr


def _slab_cols(w, nope, rope_m):
    r, nh, dd = w.shape
    return jnp.pad(w, ((0, 0), (0, 0), (0, LANES - dd))).reshape(r, nh * LANES)


def _block_diag_t(s):
    b, nh, dk, dv = s.shape
    eye = jnp.eye(nh, dtype=s.dtype)
    return jnp.einsum('bhkv,hg->bhvgk', s, eye).reshape(b, nh * dv, nh * dk)


def _unblock_t(st, nh):
    b, n, _ = st.shape
    dd = n // nh
    blocks = st.reshape(b, nh, dd, nh, dd)
    return jnp.stack([blocks[:, h, :, h, :] for h in range(nh)], axis=1).swapaxes(-1, -2)


def kernel(x_prompt, x_sample, cache_ckv, cache_krope, page_table, state_hgrn, state_ret, meta_tokens,
           norm_mix_g, w_in, q_lora_g, kv_lora_g, w_uq, w_uk, w_uv, qn_nope_g, qn_rope_g, kn_nope_g, kn_rope_g,
           hg_lb, hg_norm_g, ret_norm_g, w_o, norm_ffn_g, ffn_w_gate, ffn_w_up, ffn_w_down,
           moe_router, moe_w_gate, moe_w_up, moe_w_down):
    depth = w_in.shape[0]
    B, S, D = x_prompt.shape
    Bd, Sd, _ = x_sample.shape
    n_meta = meta_tokens.shape[0]
    nq, nkv = q_lora_g.shape[1], kv_lora_g.shape[1]
    nh, nope, rope_m = w_uq.shape[2], qn_nope_g.shape[1], qn_rope_g.shape[1]
    vdim = w_uv.shape[3]
    hg_heads, hg_dk = state_hgrn.shape[2], state_hgrn.shape[3]
    rt_heads, rt_dk = state_ret.shape[2], state_ret.shape[3]
    dh = hg_heads * hg_dk
    assert nope == 64 and rope_m == 32 and vdim == 64 and hg_dk == 64 and rt_dk == 64
    assert dh == rt_heads * rt_dk and nh % 2 == 0
    past_len = page_table.shape[1] * cache_ckv.shape[2]
    scale = float((nope + rope_m) ** -0.5 * LOG2E)
    ns = Bd * Sd

    h_b = x_prompt.reshape(B * S, D)
    h_s = jnp.concatenate([x_sample.reshape(ns, D), meta_tokens.astype(x_prompt.dtype)], axis=0)
    pos_b = jnp.tile(jnp.arange(S, dtype=F32) + n_meta, B)
    pos_s = jnp.concatenate([jnp.tile(jnp.arange(Sd, dtype=F32) + past_len, Bd), jnp.arange(n_meta, dtype=F32)])
    tabs_b = _rope_tables(pos_b, rope_m, rt_dk)
    tabs_s = _rope_tables(pos_s, rope_m, rt_dk)

    lb_soft = jax.nn.softmax(hg_lb.astype(F32), axis=0)
    lb_all = jnp.cumsum(lb_soft, axis=0) - lb_soft[0]

    lane_pad = lambda g, lo: jnp.pad(g, (lo, LANES - lo - g.shape[0]))[None, :]
    sel = jnp.zeros((LANES, rope_m), BF16).at[jnp.arange(rope_m) + 64, jnp.arange(rope_m)].set(1.0)
    cache_kropet = jnp.swapaxes(cache_krope, 2, 3)

    outs = {k: [] for k in ("ckv_p", "kr_p", "hg_p", "rt_p", "ckv_s", "kr_s", "hg_s", "rt_s")}
    for l in range(depth):
        wi = w_in[l]
        o_kr = nq + nkv
        w_perm = jnp.concatenate(
            [wi[:, :o_kr], jnp.zeros((D, 64), F32), wi[:, o_kr:o_kr + rope_m], jnp.zeros((D, LANES - 64 - rope_m), F32),
             wi[:, o_kr + rope_m:]], axis=1).astype(BF16)
        wq = _slab_cols(w_uq[l], nope, rope_m).astype(BF16)
        wk = _slab_cols(w_uk[l], nope, rope_m).astype(BF16)
        wvt = w_uv[l].reshape(nkv, nh * vdim).T.astype(BF16)
        wukt = w_uk[l].reshape(nkv, nh * nope).T.astype(BF16)
        wabs_w = jnp.einsum('rhd,hg->hdgr', w_uk[l], jnp.eye(nh, dtype=F32))
        wabs_w = jnp.pad(wabs_w, ((0, 0), (0, LANES - nope), (0, 0), (0, 0))).reshape(nh * LANES, nh * nkv).astype(BF16)
        wuv_bd = jnp.einsum('rhd,hg->hrgd', w_uv[l], jnp.eye(nh, dtype=F32)).reshape(nh * nkv, nh * vdim).astype(BF16)
        g_q = (lane_pad(qn_nope_g[l], 0) + lane_pad(qn_rope_g[l], 64)).astype(F32)
        g_k = lane_pad(kn_nope_g[l], 0).astype(F32)
        g_kr = lane_pad(kn_rope_g[l], 64).astype(F32)
        g_hg = jnp.tile(hg_norm_g[l], LANES // hg_dk)[None, :]
        g_rt = jnp.tile(ret_norm_g[l], LANES // rt_dk)[None, :]
        wo = w_o[l].astype(BF16)
        inproj = functools.partial(_inproj, gn=norm_mix_g[l][None, :], w=w_perm, gq=q_lora_g[l][None, :],
                                   gkv=kv_lora_g[l][None, :], gkr=g_kr, lb=lb_all[l][None, :],
                                   nq=nq, nkv=nkv, rope_m=rope_m, dh=dh)
        qprep = functools.partial(_qprep, w=wq, g=g_q, gk=g_k, nh=nh, nope=nope, rope_m=rope_m, scale=scale)

        cq_b, ckv_b, krs_b, hg_b, rt_b = inproj(h_b, tabs=tabs_b)
        cq_s, ckv_s, krs_s, hg_s, rt_s = inproj(h_s, tabs=tabs_s)
        (q_b,) = qprep(cq_b, cm=tabs_b[0], sm=tabs_b[1], with_qg=False)
        q_s, qg_s = qprep(cq_s, cm=tabs_s[0], sm=tabs_s[1], with_qg=True)
        k_b, vt_b = _kvup(ckv_b, krs_b, wk, wvt, g_k, nh=nh, nope=nope)
        k_m, vt_m = _kvup(ckv_s[ns:], krs_s[ns:], wk, wvt, g_k, nh=nh, nope=nope)

        o_b = _flash(q_b, k_b, vt_b, (k_m, vt_m), nh=nh, batch=B)
        mpad = LANES - n_meta
        o_m = _flash(jnp.pad(q_s[ns:], ((0, mpad), (0, 0))), jnp.pad(k_m, ((0, mpad), (0, 0))),
                     jnp.pad(vt_m, ((0, 0), (0, mpad))), None, nh=nh, batch=1)[:n_meta]
        wabs = _matmul(qg_s[:ns], wabs_w, BF16, "qabsorb").reshape(Bd, Sd * nh, nkv)
        pad_new = lambda a: jnp.pad(a.reshape(Bd, Sd, -1), ((0, 0), (0, LANES - Sd), (0, 0)))
        kr_s_compact = krs_s[:, 64:64 + rope_m]
        acc = _paged_attention(page_table, q_s[:ns].reshape(Bd, Sd * nh, LANES), wabs, wukt, sel,
                               pad_new(ckv_s[:ns]), pad_new(kr_s_compact[:ns]).swapaxes(1, 2), cache_ckv, cache_kropet, l,
                               nh=nh, nope=nope, sd=Sd)
        o_s = _matmul(acc.reshape(ns, nh * nkv), wuv_bd, F32, "vabsorb")
        o_small = jnp.concatenate([o_s, o_m], axis=0)

        zero_state = jnp.zeros((1, dh, dh), F32)
        rec = functools.partial(_recurrence, dh=dh, dk=hg_dk)
        ohg_m, sT_hg_m = rec("hgrn", hg_s[ns:][None], zero_state, chunk=SMALL_CHUNK, shared_state=True)
        ohg_s, sT_hg_s = rec("hgrn", hg_s[:ns].reshape(Bd, Sd, -1), _block_diag_t(state_hgrn[l].astype(F32)),
                             chunk=SMALL_CHUNK, shared_state=False)
        ohg_b, sT_hg_b = rec("hgrn", hg_b.reshape(B, S, -1), sT_hg_m, chunk=HG_CHUNK, shared_state=True)
        ort_m, sT_rt_m = rec("ret", rt_s[ns:][None], zero_state, chunk=SMALL_CHUNK, shared_state=True)
        ort_s, sT_rt_s = rec("ret", rt_s[:ns].reshape(Bd, Sd, -1), _block_diag_t(state_ret[l].astype(F32)),
                             chunk=SMALL_CHUNK, shared_state=False)
        ort_b, sT_rt_b = rec("ret", rt_b.reshape(B, S, -1), sT_rt_m, chunk=RET_CHUNK, shared_state=True)
        ohg_small = jnp.concatenate([ohg_s[:, :Sd].reshape(ns, dh), ohg_m[0, :n_meta]], axis=0)
        ort_small = jnp.concatenate([ort_s[:, :Sd].reshape(ns, dh), ort_m[0, :n_meta]], axis=0)

        merge = functools.partial(_merge, w_o=wo, gh=g_hg, gr=g_rt, dh=dh, dv=hg_dk)
        h_b = merge(o_b, ohg_b.reshape(B * S, dh), hg_b, ort_b.reshape(B * S, dh), rt_b, h_b)
        h_s = merge(o_small, ohg_small, hg_s, ort_small, rt_s, h_s)
        g_ffn = norm_ffn_g[l][None, :]
        i = l // 2
        if l % 2 == 0:
            mix = functools.partial(_ffn, g=g_ffn, wg=ffn_w_gate[i].astype(BF16), wu=ffn_w_up[i].astype(BF16),
                                    wd=ffn_w_down[i].astype(BF16))
        else:
            ne = moe_router.shape[2]
            wr = jnp.pad(moe_router[i], ((0, 0), (0, LANES - ne)))
            mix = functools.partial(_moe, g=g_ffn, wr=wr, wg=moe_w_gate[i].astype(BF16), wu=moe_w_up[i].astype(BF16),
                                    wd=moe_w_down[i].astype(BF16))
        h_b = mix(h_b)
        h_s = mix(h_s)

        with_meta = lambda real, meta: jnp.concatenate(
            [jnp.broadcast_to(meta[None], (B,) + meta.shape), real.reshape(B, S, -1)], axis=1)
        outs["ckv_p"].append(with_meta(ckv_b, ckv_s[ns:]))
        outs["kr_p"].append(with_meta(krs_b[:, 64:64 + rope_m], kr_s_compact[ns:]))
        outs["hg_p"].append(_unblock_t(sT_hg_b, hg_heads).astype(state_hgrn.dtype))
        outs["rt_p"].append(_unblock_t(sT_rt_b, rt_heads).astype(state_ret.dtype))
        outs["ckv_s"].append(ckv_s[:ns].reshape(Bd, Sd, nkv))
        outs["kr_s"].append(kr_s_compact[:ns].reshape(Bd, Sd, rope_m))
        outs["hg_s"].append(_unblock_t(sT_hg_s, hg_heads).astype(state_hgrn.dtype))
        outs["rt_s"].append(_unblock_t(sT_rt_s, rt_heads).astype(state_ret.dtype))

    y_prompt = h_b.reshape(B, S, D)
    y_sample = h_s[:ns].reshape(Bd, Sd, D)
    stack = lambda k: jnp.stack(outs[k])
    return (y_prompt, y_sample, stack("ckv_p"), stack("kr_p"), stack("hg_p"), stack("rt_p"),
            stack("ckv_s"), stack("kr_s"), stack("hg_s"), stack("rt_s"))
```

```python
import functools

import numpy as np
import jax
import jax.numpy as jnp
from jax import lax
from jax.experimental import pallas as pl
from jax.experimental.pallas import tpu as pltpu

F32 = jnp.float32
BF16 = jnp.bfloat16

EPS = 1e-6
ROPE_BASE = 10000.0
NEG_BIG = -1e30
LOG2E = 1.4426950408889634
LANES = 128
SUBLANES = 8
VMEM_LIMIT = 56 * 1024 * 1024

HG_CHUNK = 64
RET_CHUNK = 128
SMALL_CHUNK = 16
REC_BATCH_BLOCK = 8
PAGES_PER_STEP = 16
SCORE_BUFFERS = 3
FLASH_TILE = 1024


def _params(*sem):
    return pltpu.CompilerParams(dimension_semantics=sem, vmem_limit_bytes=VMEM_LIMIT)


def _row_tile(n, pref):
    if n <= pref:
        return n
    t = pref - pref % 16
    while t >= 16:
        if n % t == 0:
            return t
        t -= 16
    return n


def _dot(a, b):
    return jnp.dot(a.astype(BF16), b.astype(BF16), preferred_element_type=F32)


def _dot_nt(a, b):
    return lax.dot_general(a.astype(BF16), b.astype(BF16), (((1,), (1,)), ((), ())), preferred_element_type=F32)


def _dot_tn(a, b):
    return lax.dot_general(a.astype(BF16), b.astype(BF16), (((0,), (0,)), ((), ())), preferred_element_type=F32)


def _rms(x, width):
    return x * lax.rsqrt(jnp.sum(x * x, axis=-1, keepdims=True) * (1.0 / width) + EPS)


def _rope128(y, cos, sin_signed, half):
    lane = lax.broadcasted_iota(jnp.int32, y.shape, 1)
    first = (lane % (2 * half)) < half
    rot = jnp.where(first, pltpu.roll(y, LANES - half, 1), pltpu.roll(y, half, 1))
    return y * cos + rot * sin_signed


def _sigmoid(x):
    return 1.0 / (1.0 + jnp.exp(-x))


def _inproj_kernel(h_ref, gn_ref, w_ref, gq_ref, gkv_ref, gkr_ref, lb_ref, wq_ref, gqs_ref, gks_ref,
                   cm_ref, sm_ref, cr_ref, sr_ref, *out_refs, nq, nkv, rope_m, dh, nh, nope, scale):
    q_refs = out_refs[:-4]
    ckv_ref, kr_ref, hg_ref, rt_ref = out_refs[-4:]
    x = h_ref[...]
    xb = (_rms(x, x.shape[-1]) * gn_ref[...]).astype(BF16)

    def proj(lo, width):
        return jnp.dot(xb, w_ref[:, lo:lo + width], preferred_element_type=F32)

    cm = cm_ref[...]
    sm = sm_ref[...]
    c = proj(0, nq)
    cq = (_rms(c, nq) * gq_ref[...]).astype(BF16)
    gqs = gqs_ref[...]
    for h in range(nh):
        x = jnp.dot(cq, wq_ref[:, h * LANES:(h + 1) * LANES], preferred_element_type=F32)
        lane = lax.broadcasted_iota(jnp.int32, x.shape, 1)
        is_nope = lane < nope
        x2 = x * x
        ss_n = jnp.sum(jnp.where(is_nope, x2, 0.0), axis=-1, keepdims=True)
        ss_r = jnp.sum(jnp.where(is_nope, 0.0, x2), axis=-1, keepdims=True)
        inv = jnp.where(is_nope, lax.rsqrt(ss_n * (1.0 / nope) + EPS), lax.rsqrt(ss_r * (1.0 / rope_m) + EPS))
        y = _rope128(x * inv * gqs, cm, sm, rope_m // 2) * scale
        q_refs[0][:, h * LANES:(h + 1) * LANES] = y.astype(BF16)
        if len(q_refs) > 1:
            q_refs[1][:, h * LANES:(h + 1) * LANES] = (y * gks_ref[...]).astype(BF16)

    c = proj(nq, nkv)
    ckv_ref[...] = _rms(c, nkv) * gkv_ref[...]
    k = proj(nq + nkv, LANES)
    k = _rms(k, rope_m) * gkr_ref[...]
    kr_ref[...] = _rope128(k, cm, sm, rope_m // 2)

    base = nq + nkv + LANES
    hq = proj(base, dh)
    hg_ref[:, 0:dh] = hq * _sigmoid(hq)
    z = proj(base + dh, dh)
    lb = lb_ref[...]
    f = lb + (1.0 - lb) * _sigmoid(z)
    hg_ref[:, dh:2 * dh] = 1.0 - f
    hg_ref[:, 2 * dh:3 * dh] = jnp.log(f)
    hg_ref[:, 3 * dh:4 * dh] = proj(base + 2 * dh, dh)
    hg_ref[:, 4 * dh:5 * dh] = proj(base + 3 * dh, dh)

    base = base + 4 * dh
    cr = cr_ref[...]
    sr = sr_ref[...]
    for part, scale in ((0, 1.0), (1, 64 ** -0.5)):
        for s in range(dh // LANES):
            lo = part * dh + s * LANES
            y = _rope128(proj(base + lo, LANES), cr, sr, 32)
            rt_ref[:, lo:lo + LANES] = y * scale
    rt_ref[:, 2 * dh:3 * dh] = proj(base + 2 * dh, dh)
    rt_ref[:, 3 * dh:4 * dh] = proj(base + 3 * dh, dh)


def _inproj(h, gn, w, gq, gkv, gkr, lb, wq, gqs, gks, tabs, *, nq, nkv, rope_m, dh, nh, nope, scale, with_qg):
    n, d = h.shape
    tm = _row_tile(n, 512)
    cm, sm, cr, sr = tabs
    assert cm.shape[0] % tm == 0
    period = cm.shape[0] // tm
    row = lambda width: pl.BlockSpec((tm, width), lambda i: (i, 0))
    tab = pl.BlockSpec((tm, LANES), lambda i: (i % period, 0))
    const = lambda a: pl.BlockSpec(a.shape, lambda i: (0, 0))
    nqo = 2 if with_qg else 1
    return pl.pallas_call(
        functools.partial(_inproj_kernel, nq=nq, nkv=nkv, rope_m=rope_m, dh=dh, nh=nh, nope=nope, scale=scale),
        grid=(n // tm,),
        in_specs=[row(d), const(gn), const(w), const(gq), const(gkv), const(gkr), const(lb), const(wq), const(gqs),
                  const(gks), tab, tab, tab, tab],
        out_specs=[row(nh * LANES)] * nqo + [row(nkv), row(LANES), row(5 * dh), row(4 * dh)],
        out_shape=[jax.ShapeDtypeStruct((n, nh * LANES), BF16)] * nqo
                  + [jax.ShapeDtypeStruct((n, nkv), F32), jax.ShapeDtypeStruct((n, LANES), F32),
                     jax.ShapeDtypeStruct((n, 5 * dh), F32), jax.ShapeDtypeStruct((n, 4 * dh), F32)],
        compiler_params=_params("parallel"),
        name="inproj",
    )(h, gn, w, gq, gkv, gkr, lb, wq, gqs, gks, cm, sm, cr, sr)


def _kvup_kernel(c_ref, kr_ref, wk_ref, wvt_ref, gk_ref, k_ref, vt_ref, *, nh, nope):
    c = c_ref[...].astype(BF16)
    kr = kr_ref[...]
    gk = gk_ref[...]
    for h in range(nh):
        x = jnp.dot(c, wk_ref[:, h * LANES:(h + 1) * LANES], preferred_element_type=F32)
        k_ref[:, h * LANES:(h + 1) * LANES] = (_rms(x, nope) * gk + kr).astype(BF16)
    vt_ref[...] = _dot_nt(wvt_ref[...], c).astype(BF16)


def _kvup(ckv, krs, wk, wvt, gk, *, nh, nope):
    n, r = ckv.shape
    tm = _row_tile(n, 512)
    row = lambda width: pl.BlockSpec((tm, width), lambda i: (i, 0))
    const = lambda a: pl.BlockSpec(a.shape, lambda i: (0, 0))
    nv = wvt.shape[0]
    return pl.pallas_call(
        functools.partial(_kvup_kernel, nh=nh, nope=nope),
        grid=(n // tm,),
        in_specs=[row(r), row(LANES), const(wk), const(wvt), const(gk)],
        out_specs=[row(nh * LANES), pl.BlockSpec((nv, tm), lambda i: (0, i))],
        out_shape=[jax.ShapeDtypeStruct((n, nh * LANES), BF16), jax.ShapeDtypeStruct((nv, n), BF16)],
        compiler_params=_params("parallel"),
        name="kvup",
    )(ckv, krs, wk, wvt, gk)


def _flash_kernel(*refs, nh, vdim, has_prefix):
    if has_prefix:
        q_ref, k_ref, vt_ref, kp_ref, vtp_ref, o_ref, m_sc, l_sc, acc_sc, s_sc = refs
    else:
        q_ref, k_ref, vt_ref, o_ref, m_sc, l_sc, acc_sc, s_sc = refs
    qi = pl.program_id(1)
    ki = pl.program_id(2)

    def update(k_r, vt_r, masked):
        tk = k_r.shape[0]

        def scores(h):
            st = _dot_nt(k_r[:, h * LANES:(h + 1) * LANES], q_ref[:, h * LANES:(h + 1) * LANES])
            if masked:
                key = lax.broadcasted_iota(jnp.int32, st.shape, 0)
                qry = lax.broadcasted_iota(jnp.int32, st.shape, 1)
                st = jnp.where(key <= qry, st, NEG_BIG)
            s_sc[h % nbuf, 0:tk, :] = st

        nbuf = s_sc.shape[0]
        for h in range(nbuf - 1):
            scores(h)
        for h in range(nh):
            if h + nbuf - 1 < nh:
                scores(h + nbuf - 1)
            rows = slice(h * vdim, (h + 1) * vdim)
            st = s_sc[h % nbuf, 0:tk, :]
            m_prev = m_sc[h:h + 1, :]
            m_new = jnp.maximum(m_prev, jnp.max(st, axis=0, keepdims=True))
            alpha = jnp.exp2(m_prev - m_new)
            e = jnp.exp2(st - m_new)
            l_sc[h:h + 1, :] = alpha * l_sc[h:h + 1, :] + jnp.sum(e, axis=0, keepdims=True)
            m_sc[h:h + 1, :] = m_new
            pv = jnp.dot(vt_r[rows, :], e.astype(BF16), preferred_element_type=F32)
            acc_sc[rows, :] = acc_sc[rows, :] * alpha + pv

    @pl.when(ki == 0)
    def _():
        m_sc[...] = jnp.full(m_sc.shape, NEG_BIG, F32)
        l_sc[...] = jnp.zeros(l_sc.shape, F32)
        acc_sc[...] = jnp.zeros(acc_sc.shape, F32)
        if has_prefix:
            update(kp_ref, vtp_ref, False)

    @pl.when(ki < qi)
    def _():
        update(k_ref, vt_ref, False)

    @pl.when(ki == qi)
    def _():
        update(k_ref, vt_ref, True)
        for h in range(nh):
            rows = slice(h * vdim, (h + 1) * vdim)
            acc_sc[rows, :] = acc_sc[rows, :] / l_sc[h:h + 1, :]
        o_ref[...] = acc_sc[...].T


def _flash(q, k, vt, prefix, *, nh, batch):
    n = q.shape[0]
    t = n // batch
    dv = vt.shape[0]
    tq = _row_tile(t, FLASH_TILE)
    nt = t // tq
    qspec = pl.BlockSpec((tq, nh * LANES), lambda bi, qi, ki: (bi * nt + qi, 0))
    kspec = pl.BlockSpec((tq, nh * LANES), lambda bi, qi, ki: (bi * nt + jnp.minimum(ki, qi), 0))
    vspec = pl.BlockSpec((dv, tq), lambda bi, qi, ki: (0, bi * nt + jnp.minimum(ki, qi)))
    ins, specs = [q, k, vt], [qspec, kspec, vspec]
    if prefix is not None:
        for a in prefix:
            ins.append(a)
            specs.append(pl.BlockSpec(a.shape, lambda bi, qi, ki: (0, 0)))
    return pl.pallas_call(
        functools.partial(_flash_kernel, nh=nh, vdim=dv // nh, has_prefix=prefix is not None),
        grid=(batch, nt, nt),
        in_specs=specs,
        out_specs=pl.BlockSpec((tq, dv), lambda bi, qi, ki: (bi * nt + qi, 0)),
        out_shape=jax.ShapeDtypeStruct((n, dv), F32),
        scratch_shapes=[pltpu.VMEM((nh, tq), F32), pltpu.VMEM((nh, tq), F32), pltpu.VMEM((dv, tq), F32),
                        pltpu.VMEM((SCORE_BUFFERS, tq, tq), F32)],
        compiler_params=_params("parallel", "parallel", "arbitrary"),
        name="flash_prefix" if prefix is not None else "flash",
    )(*ins)


def _matmul_kernel(a_ref, w_ref, o_ref):
    o_ref[...] = jnp.dot(a_ref[...].astype(BF16), w_ref[...], preferred_element_type=F32).astype(o_ref.dtype)


def _matmul(a, w, out_dtype, name):
    n, k = a.shape
    tm = _row_tile(n, 512)
    return pl.pallas_call(
        _matmul_kernel,
        grid=(n // tm,),
        in_specs=[pl.BlockSpec((tm, k), lambda i: (i, 0)), pl.BlockSpec(w.shape, lambda i: (0, 0))],
        out_specs=pl.BlockSpec((tm, w.shape[1]), lambda i: (i, 0)),
        out_shape=jax.ShapeDtypeStruct((n, w.shape[1]), out_dtype),
        compiler_params=_params("parallel"),
        name=name,
    )(a, w)


def _paged_kernel(pt_ref, q_ref, wabs_ref, wukt_ref, sel_ref, cn_ref, krn_ref, ckv_hbm, krt_hbm, o_ref,
                  lhs_sc, cbuf, kbuf, sem, *, layer, npg, nchunk, sub, nh, nope, sd):
    b = pl.program_id(0)
    nb = pl.num_programs(0)
    nk = nh * nope
    rows = sd * nh
    page = ckv_hbm.shape[2]
    nslot = cbuf.shape[0]

    def chunk_slot(off):
        return lax.rem(b * nchunk + off, nslot)

    def page_copies(off, p):
        slot = chunk_slot(off)
        idx = pt_ref[b + off // nchunk, (off % nchunk) * npg + p]
        span = pl.ds(p * page, page)
        return (pltpu.make_async_copy(ckv_hbm.at[layer, idx], cbuf.at[slot, span, :], sem.at[0, slot]),
                pltpu.make_async_copy(krt_hbm.at[layer, idx], kbuf.at[slot, :, span], sem.at[1, slot]))

    def start_chunk(off):
        for p in range(npg):
            for cp in page_copies(off, p):
                cp.start()

    def wait_chunk(off):
        for p in range(npg):
            for cp in page_copies(off, p):
                cp.wait()

    @pl.when(b == 0)
    def _():
        start_chunk(0)
        start_chunk(1)

    lhs_sc[0:nk, :] = wukt_ref[...]
    lhs_sc[nk:nk + rows, :] = wabs_ref[0]
    qr = jnp.dot(q_ref[0], sel_ref[...], preferred_element_type=F32).astype(BF16)

    def scores(c, krt):
        big = _dot_nt(lhs_sc[...], c)
        kraw = big[0:nk]
        ssq = jnp.sum((kraw * kraw).reshape(nh, nope, kraw.shape[-1]), axis=1)
        r = lax.rsqrt(ssq * (1.0 / nope) + EPS)
        return big[nk:nk + rows] * jnp.concatenate([r] * sd, axis=0) + jnp.dot(qr, krt, preferred_element_type=F32)

    def update(state, s, c):
        m_prev, l_prev, acc = state
        m_new = jnp.maximum(m_prev, jnp.max(s, axis=-1, keepdims=True))
        alpha = jnp.exp2(m_prev - m_new)
        e = jnp.exp2(s - m_new)
        l_new = alpha * l_prev + jnp.sum(e, axis=-1, keepdims=True)
        return m_new, l_new, acc * alpha + jnp.dot(e.astype(BF16), c, preferred_element_type=F32)

    def chunk_scores(off):
        slot = chunk_slot(off)
        span = sub * page
        cs, ss = [], []
        for g in range(npg // sub):
            lat = cbuf[slot, g * span:(g + 1) * span, :].astype(BF16)
            cs.append(lat)
            ss.append(scores(lat, kbuf[slot, :, g * span:(g + 1) * span].astype(BF16)))
        return jnp.concatenate(ss, axis=1), jnp.concatenate(cs, axis=0)

    state = (jnp.full((rows, 1), NEG_BIG, F32), jnp.zeros((rows, 1), F32), jnp.zeros((rows, cbuf.shape[2]), F32))
    wait_chunk(0)
    cur = chunk_scores(0)
    for c in range(nchunk):
        if c + 2 < nchunk:
            start_chunk(c + 2)
        else:
            @pl.when(b + 1 < nb)
            def _():
                start_chunk(c + 2)
        if c + 1 < nchunk:
            wait_chunk(c + 1)
            nxt = chunk_scores(c + 1)
        state = update(state, *cur)
        if c + 1 < nchunk:
            cur = nxt

    lat = cn_ref[0].astype(BF16)
    s = scores(lat, krn_ref[0].astype(BF16))
    row = lax.broadcasted_iota(jnp.int32, s.shape, 0)
    col = lax.broadcasted_iota(jnp.int32, s.shape, 1)
    _, l_fin, acc = update(state, jnp.where(col * nh <= row, s, NEG_BIG), lat)
    o_ref[0] = acc / l_fin


def _paged_attention(page_table, q, wabs, wukt, sel, cn, krnt, cache_ckv, cache_kropet, layer, *, nh, nope, sd):
    bd, rows, _ = q.shape
    n_pages = page_table.shape[1]
    assert n_pages % 2 == 0
    npg = max(p for p in range(1, PAGES_PER_STEP + 1) if n_pages % (2 * p) == 0)
    nchunk = n_pages // npg
    nslot = 3
    sub = 2 if npg % 2 == 0 else 1
    page, kvl = cache_ckv.shape[2:]
    rp = cache_kropet.shape[2]
    per_b = lambda a: pl.BlockSpec((1,) + a.shape[1:], lambda b, pt: (b, 0, 0))
    const = lambda a: pl.BlockSpec(a.shape, lambda b, pt: (0, 0))
    hbm = pl.BlockSpec(memory_space=pl.ANY)
    grid_spec = pltpu.PrefetchScalarGridSpec(
        num_scalar_prefetch=1,
        grid=(bd,),
        in_specs=[per_b(q), per_b(wabs), const(wukt), const(sel), per_b(cn), per_b(krnt), hbm, hbm],
        out_specs=pl.BlockSpec((1, rows, kvl), lambda b, pt: (b, 0, 0)),
        scratch_shapes=[pltpu.VMEM((nh * nope + rows, kvl), BF16), pltpu.VMEM((nslot, npg * page, kvl), F32),
                        pltpu.VMEM((nslot, rp, npg * page), F32), pltpu.SemaphoreType.DMA((2, nslot))],
    )
    return pl.pallas_call(
        functools.partial(_paged_kernel, layer=layer, npg=npg, nchunk=nchunk, sub=sub, nh=nh, nope=nope, sd=sd),
        grid_spec=grid_spec,
        out_shape=jax.ShapeDtypeStruct((bd, rows, kvl), F32),
        compiler_params=_params("arbitrary"),
        name="paged_attention",
    )(page_table, q, wabs, wukt, sel, cn, krnt, cache_ckv, cache_kropet)


def _block_mask(n, blk):
    r = lax.broadcasted_iota(jnp.int32, (n, n), 0) // blk
    c = lax.broadcasted_iota(jnp.int32, (n, n), 1) // blk
    return r == c


def _split3(x):
    hi = x.astype(BF16)
    r1 = x - hi.astype(F32)
    mid = r1.astype(BF16)
    lo = (r1 - mid.astype(F32)).astype(BF16)
    return hi, mid, lo


def _hgrn_kernel(x_ref, s0_ref, o_ref, sT_ref, st_sc, pad_sc, sh_sc, e_sc, *, dh, dk, chunk, bb):
    for i in range(bb):
        _hgrn_one(x_ref.at[i], s0_ref.at[i % s0_ref.shape[0]], o_ref.at[i], sT_ref.at[i],
                  st_sc.at[i], pad_sc.at[i], sh_sc.at[i], e_sc.at[i], dh=dh, dk=dk, chunk=chunk)


def _hgrn_one(x_ref, s0_ref, o_ref, sT_ref, st_sc, pad_sc, sh_sc, e_sc, *, dh, dk, chunk):
    ci = pl.program_id(1)

    @pl.when(ci == 0)
    def _():
        st_sc[...] = s0_ref[...]
        pad_sc[:, 0:SUBLANES, :] = jnp.zeros((3, SUBLANES, dh), F32)

    q = x_ref[:, 0:dh]
    k = x_ref[:, dh:2 * dh]
    g = x_ref[:, 2 * dh:3 * dh]
    v = x_ref[:, 3 * dh:4 * dh]
    r = lax.broadcasted_iota(jnp.int32, (chunk, chunk), 0)
    c = lax.broadcasted_iota(jnp.int32, (chunk, chunk), 1)
    tri = jnp.where(c <= r, 1.0, 0.0).astype(BF16)
    b = sum(jnp.dot(tri, part, preferred_element_type=F32) for part in _split3(g * LOG2E))
    for i, val in enumerate((k, b, v)):
        pad_sc[i, SUBLANES:SUBLANES + chunk, :] = val
        sh_sc[i, 0] = val
        for r in range(1, SUBLANES):
            sh_sc[i, r] = pad_sc[i, SUBLANES - r:SUBLANES - r + chunk, :]
    ones_bd = jnp.where(_block_mask(dh, dk), 1.0, 0.0).astype(BF16)

    o_ref[...] = _dot_nt(q * jnp.exp2(b), st_sc[...])
    bands = []
    off = 0
    for a in range(chunk // SUBLANES):
        lo = a * SUBLANES
        rows = chunk - lo
        qa = q[lo:, :]
        ba = b[lo:, :]
        for r in range(SUBLANES):
            e_sc[off:off + rows, :] = qa * sh_sc[0, r, 0:rows, :] * jnp.exp2(ba - sh_sc[1, r, 0:rows, :])
            bands.append((lo, rows, r, off))
            off += rows
    e_sc[...] = jnp.dot(e_sc[...].astype(BF16), ones_bd, preferred_element_type=F32)
    for a in range(chunk // SUBLANES):
        acc = None
        for lo, rows, r, off in bands[a * SUBLANES:(a + 1) * SUBLANES]:
            term = e_sc[off:off + rows, :] * sh_sc[2, r, 0:rows, :]
            acc = term if acc is None else acc + term
        o_ref[lo:, :] += acc

    b_last = b[chunk - 1:chunk, :]
    upd = _dot_tn(v, k * jnp.exp2(b_last - b))
    st_sc[...] = st_sc[...] * jnp.exp2(b_last) + jnp.where(_block_mask(dh, dk), upd, 0.0)

    @pl.when(ci == pl.num_programs(1) - 1)
    def _():
        sT_ref[...] = st_sc[...]


def _ret_kernel(x_ref, s0_ref, o_ref, sT_ref, st_sc, *, bb, **kw):
    for i in range(bb):
        _ret_one(x_ref.at[i], s0_ref.at[i % s0_ref.shape[0]], o_ref.at[i], sT_ref.at[i], st_sc.at[i], **kw)


def _ret_one(x_ref, s0_ref, o_ref, sT_ref, st_sc, *, dh, dk, chunk, n_valid, log_gamma):
    ci = pl.program_id(1)

    @pl.when(ci == 0)
    def _():
        st_sc[...] = s0_ref[...]

    q = x_ref[:, 0:dh]
    k = x_ref[:, dh:2 * dh]
    v = x_ref[:, 2 * dh:3 * dh]
    lane = lax.broadcasted_iota(jnp.int32, (chunk, dh), 1)
    head = lane // dk
    lg = jnp.zeros((chunk, dh), F32)
    for h, val in enumerate(log_gamma):
        lg = jnp.where(head == h, val, lg)
    t = lax.broadcasted_iota(jnp.int32, (chunk, dh), 0).astype(F32)
    r = lax.broadcasted_iota(jnp.int32, (chunk, chunk), 0)
    c = lax.broadcasted_iota(jnp.int32, (chunk, chunk), 1)
    diff = (r - c).astype(F32)

    o = _dot_nt(q, st_sc[...]) * jnp.exp((t + 1.0) * lg)
    for h, val in enumerate(log_gamma):
        in_h = head == h
        a = _dot_nt(jnp.where(in_h, q, 0.0), k)
        a = a * jnp.where(diff >= 0, jnp.exp(jnp.maximum(diff, 0.0) * val), 0.0)
        o = o + jnp.where(in_h, _dot(a, v), 0.0)
    o_ref[...] = o

    upd = _dot_tn(v, k * jnp.exp((n_valid - 1.0 - t) * lg))
    st_sc[...] = st_sc[...] * jnp.exp(n_valid * lg[0:1, :]) + jnp.where(_block_mask(dh, dk), upd, 0.0)

    @pl.when(ci == pl.num_programs(1) - 1)
    def _():
        sT_ref[...] = st_sc[...]


def _recurrence(kind, x, s0, *, dh, dk, chunk, shared_state):
    b, t, w = x.shape
    n_valid = min(chunk, t)
    if t < chunk:
        x = jnp.pad(x, ((0, 0), (0, chunk - t), (0, 0)))
    t_pad = x.shape[1]
    bb = max(d for d in range(1, REC_BATCH_BLOCK + 1) if b % d == 0)
    s0_spec = (pl.BlockSpec((1, dh, dh), lambda bi, ci: (0, 0, 0)) if shared_state
               else pl.BlockSpec((bb, dh, dh), lambda bi, ci: (bi, 0, 0)))
    if kind == "hgrn":
        body = functools.partial(_hgrn_kernel, dh=dh, dk=dk, chunk=chunk, bb=bb)
        band_rows = SUBLANES * sum(range(SUBLANES, chunk + 1, SUBLANES))
        scratch = [pltpu.VMEM((bb, dh, dh), F32), pltpu.VMEM((bb, 3, SUBLANES + chunk, dh), F32),
                   pltpu.VMEM((bb, 3, SUBLANES, chunk, dh), F32), pltpu.VMEM((bb, band_rows, dh), F32)]
    else:
        log_gamma = tuple(float(np.log1p(-np.exp2(-5.0 - h))) for h in range(dh // dk))
        body = functools.partial(_ret_kernel, bb=bb, dh=dh, dk=dk, chunk=chunk, n_valid=n_valid,
                                 log_gamma=log_gamma)
        scratch = [pltpu.VMEM((bb, dh, dh), F32)]
    return pl.pallas_call(
        body,
        grid=(b // bb, t_pad // chunk),
        in_specs=[pl.BlockSpec((bb, chunk, w), lambda bi, ci: (bi, ci, 0)), s0_spec],
        out_specs=[pl.BlockSpec((bb, chunk, dh), lambda bi, ci: (bi, ci, 0)),
                   pl.BlockSpec((bb, dh, dh), lambda bi, ci: (bi, 0, 0))],
        out_shape=[jax.ShapeDtypeStruct((b, t_pad, dh), F32), jax.ShapeDtypeStruct((b, dh, dh), F32)],
        scratch_shapes=scratch,
        compiler_params=_params("parallel", "arbitrary"),
        name=kind,
    )(x, s0)


def _merge_kernel(om_ref, oh_ref, hg_ref, or_ref, rg_ref, h_ref, w_ref, gh_ref, gr_ref, o_ref, *, nm, dh, dv):
    def gated(o_r, gate_r, g_r):
        o = o_r[...]
        gate = gate_r[...]
        out = []
        for s in range(dh // LANES):
            x = o[:, s * LANES:(s + 1) * LANES]
            lane = lax.broadcasted_iota(jnp.int32, x.shape, 1)
            low = lane < dv
            x2 = x * x
            ss_l = jnp.sum(jnp.where(low, x2, 0.0), axis=-1, keepdims=True)
            ss_h = jnp.sum(jnp.where(low, 0.0, x2), axis=-1, keepdims=True)
            inv = lax.rsqrt(jnp.where(low, ss_l, ss_h) * (1.0 / dv) + EPS)
            gt = gate[:, s * LANES:(s + 1) * LANES]
            out.append(x * inv * g_r[...] * (gt * _sigmoid(gt)))
        return jnp.concatenate(out, axis=-1)

    y = _dot(om_ref[...], w_ref[0:nm, :])
    y = y + _dot(gated(oh_ref, hg_ref, gh_ref), w_ref[nm:nm + dh, :])
    y = y + _dot(gated(or_ref, rg_ref, gr_ref), w_ref[nm + dh:nm + 2 * dh, :])
    o_ref[...] = h_ref[...] + y


def _merge(o_mla, o_hg, hg_pack, o_rt, rt_pack, h, w_o, gh, gr, *, dh, dv):
    n, d = h.shape
    nm = o_mla.shape[1]
    tm = _row_tile(n, 512)
    row = lambda width, cb=0: pl.BlockSpec((tm, width), lambda i: (i, cb))
    const = lambda a: pl.BlockSpec(a.shape, lambda i: (0, 0))
    return pl.pallas_call(
        functools.partial(_merge_kernel, nm=nm, dh=dh, dv=dv),
        grid=(n // tm,),
        in_specs=[row(nm), row(dh), row(dh, 4), row(dh), row(dh, 3), row(d), const(w_o), const(gh), const(gr)],
        out_specs=row(d),
        out_shape=jax.ShapeDtypeStruct((n, d), F32),
        compiler_params=_params("parallel"),
        name="merge",
    )(o_mla, o_hg, hg_pack, o_rt, rt_pack, h, w_o, gh, gr)


def _ffn_kernel(h_ref, g_ref, wg_ref, wu_ref, wd_ref, o_ref, xn_sc, acc_sc):
    j = pl.program_id(1)

    @pl.when(j == 0)
    def _():
        x = h_ref[...]
        xn_sc[...] = (_rms(x, x.shape[-1]) * g_ref[...]).astype(BF16)
        acc_sc[...] = jnp.zeros(acc_sc.shape, F32)

    xn = xn_sc[...]
    a = jnp.dot(xn, wg_ref[...], preferred_element_type=F32)
    u = jnp.dot(xn, wu_ref[...], preferred_element_type=F32)
    acc_sc[...] += jnp.dot((a * _sigmoid(a) * u).astype(BF16), wd_ref[...], preferred_element_type=F32)

    @pl.when(j == pl.num_programs(1) - 1)
    def _():
        o_ref[...] = h_ref[...] + acc_sc[...]


def _ffn(h, g, wg, wu, wd):
    n, d = h.shape
    ff = wg.shape[1]
    tm = _row_tile(n, 512)
    tf = ff // 2 if (ff // 2) % LANES == 0 else ff
    return pl.pallas_call(
        _ffn_kernel,
        grid=(n // tm, ff // tf),
        in_specs=[pl.BlockSpec((tm, d), lambda i, j: (i, 0)), pl.BlockSpec(g.shape, lambda i, j: (0, 0)),
                  pl.BlockSpec((d, tf), lambda i, j: (0, j)), pl.BlockSpec((d, tf), lambda i, j: (0, j)),
                  pl.BlockSpec((tf, d), lambda i, j: (j, 0))],
        out_specs=pl.BlockSpec((tm, d), lambda i, j: (i, 0)),
        out_shape=jax.ShapeDtypeStruct((n, d), F32),
        scratch_shapes=[pltpu.VMEM((tm, d), BF16), pltpu.VMEM((tm, d), F32)],
        compiler_params=_params("parallel", "arbitrary"),
        name="ffn",
    )(h, g, wg, wu, wd)


def _moe_kernel(h_ref, g_ref, wr_ref, wg_ref, wu_ref, wd_ref, o_ref, xn_sc, gate_sc, acc_sc, *, ne):
    e = pl.program_id(1)

    @pl.when(e == 0)
    def _():
        x = h_ref[...]
        xn = _rms(x, x.shape[-1]) * g_ref[...]
        xn_sc[...] = xn.astype(BF16)
        acc_sc[...] = jnp.zeros(acc_sc.shape, F32)
        xh, xm, _ = _split3(xn)
        wh, wm, _ = _split3(wr_ref[...])
        dd = lambda a, b: jnp.dot(a, b, preferred_element_type=F32)
        logits = dd(xh, wh) + (dd(xh, wm) + dd(xm, wh))
        lane = lax.broadcasted_iota(jnp.int32, logits.shape, 1)
        valid = lane < ne
        lg = jnp.where(valid, logits, -jnp.inf)
        v1 = jnp.max(lg, axis=-1, keepdims=True)
        i1 = jnp.min(jnp.where(lg == v1, lane, LANES), axis=-1, keepdims=True)
        lg2 = jnp.where(lane == i1, -jnp.inf, lg)
        v2 = jnp.max(lg2, axis=-1, keepdims=True)
        i2 = jnp.min(jnp.where(lg2 == v2, lane, LANES), axis=-1, keepdims=True)
        e2 = jnp.exp(v2 - v1)
        w1 = 1.0 / (1.0 + e2)
        w2 = e2 / (1.0 + e2)
        gate_sc[...] = jnp.where(lane == i1, w1, jnp.where(lane == i2, w2, 0.0))

    lane = lax.broadcasted_iota(jnp.int32, gate_sc.shape, 1)
    gate = jnp.sum(jnp.where(lane == e, gate_sc[...], 0.0), axis=-1, keepdims=True)
    xn = xn_sc[...]
    a = jnp.dot(xn, wg_ref[0], preferred_element_type=F32)
    u = jnp.dot(xn, wu_ref[0], preferred_element_type=F32)
    acc_sc[...] += gate * jnp.dot((a * _sigmoid(a) * u).astype(BF16), wd_ref[0], preferred_element_type=F32)

    @pl.when(e == ne - 1)
    def _():
        o_ref[...] = h_ref[...] + acc_sc[...]


def _moe(h, g, wr, wg, wu, wd):
    n, d = h.shape
    ne, _, ff = wg.shape
    tm = _row_tile(n, 512)
    return pl.pallas_call(
        functools.partial(_moe_kernel, ne=ne),
        grid=(n // tm, ne),
        in_specs=[pl.BlockSpec((tm, d), lambda i, e: (i, 0)), pl.BlockSpec(g.shape, lambda i, e: (0, 0)),
                  pl.BlockSpec(wr.shape, lambda i, e: (0, 0)),
                  pl.BlockSpec((1, d, ff), lambda i, e: (e, 0, 0)), pl.BlockSpec((1, d, ff), lambda i, e: (e, 0, 0)),
                  pl.BlockSpec((1, ff, d), lambda i, e: (e, 0, 0))],
        out_specs=pl.BlockSpec((tm, d), lambda i, e: (i, 0)),
        out_shape=jax.ShapeDtypeStruct((n, d), F32),
        scratch_shapes=[pltpu.VMEM((tm, d), BF16), pltpu.VMEM((tm, LANES), F32), pltpu.VMEM((tm, d), F32)],
        compiler_params=_params("parallel", "arbitrary"),
        name="moe",
    )(h, g, wr, wg, wu, wd)


def _rope_tables(pos, rope_m, ret_dk):
    def angles(d):
        inv = ROPE_BASE ** (-jnp.arange(0, d, 2, dtype=F32) / d)
        return pos.astype(F32)[:, None] * inv[None, :]

    n = pos.shape[0]
    a = angles(rope_m)
    c, s = jnp.cos(a), jnp.sin(a)
    pad = LANES - 64 - rope_m
    cm = jnp.concatenate([jnp.ones((n, 64), F32), c, c, jnp.ones((n, pad), F32)], axis=1)
    sm = jnp.concatenate([jnp.zeros((n, 64), F32), -s, s, jnp.zeros((n, pad), F32)], axis=1)
    a = angles(ret_dk)
    c, s = jnp.cos(a), jnp.sin(a)
    reps = LANES // ret_dk
    cr = jnp.concatenate([c, c] * reps, axis=1)
    sr = jnp.concatenate([-s, s] * reps, axis=1)
    return cm, sm, cr, sr


def _slab_cols(w, nope, rope_m):
    r, nh, dd = w.shape
    return jnp.pad(w, ((0, 0), (0, 0), (0, LANES - dd))).reshape(r, nh * LANES)


def _block_diag_t(s):
    b, nh, dk, dv = s.shape
    out = jnp.zeros((b, nh * dv, nh * dk), s.dtype)
    for h in range(nh):
        out = out.at[:, h * dv:(h + 1) * dv, h * dk:(h + 1) * dk].set(s[:, h].swapaxes(-1, -2))
    return out


def _unblock_t(st, nh):
    b, n, _ = st.shape
    dd = n // nh
    return jnp.stack([st[:, h * dd:(h + 1) * dd, h * dd:(h + 1) * dd].swapaxes(-1, -2) for h in range(nh)], axis=1)


def kernel(x_prompt, x_sample, cache_ckv, cache_krope, page_table, state_hgrn, state_ret, meta_tokens,
           norm_mix_g, w_in, q_lora_g, kv_lora_g, w_uq, w_uk, w_uv, qn_nope_g, qn_rope_g, kn_nope_g, kn_rope_g,
           hg_lb, hg_norm_g, ret_norm_g, w_o, norm_ffn_g, ffn_w_gate, ffn_w_up, ffn_w_down,
           moe_router, moe_w_gate, moe_w_up, moe_w_down):
    depth = w_in.shape[0]
    B, S, D = x_prompt.shape
    Bd, Sd, _ = x_sample.shape
    n_meta = meta_tokens.shape[0]
    nq, nkv = q_lora_g.shape[1], kv_lora_g.shape[1]
    nh, nope, rope_m = w_uq.shape[2], qn_nope_g.shape[1], qn_rope_g.shape[1]
    vdim = w_uv.shape[3]
    hg_heads, hg_dk = state_hgrn.shape[2], state_hgrn.shape[3]
    rt_heads, rt_dk = state_ret.shape[2], state_ret.shape[3]
    dh = hg_heads * hg_dk
    assert nope == 64 and rope_m == 32 and vdim == 64 and hg_dk == 64 and rt_dk == 64
    assert dh == rt_heads * rt_dk and nh % 2 == 0
    past_len = page_table.shape[1] * cache_ckv.shape[2]
    scale = float((nope + rope_m) ** -0.5 * LOG2E)
    ns = Bd * Sd

    h_b = x_prompt.reshape(B * S, D)
    h_s = jnp.concatenate([x_sample.reshape(ns, D), meta_tokens.astype(x_prompt.dtype)], axis=0)
    pos_b = jnp.arange(S, dtype=F32) + n_meta
    pos_s = jnp.concatenate([jnp.tile(jnp.arange(Sd, dtype=F32) + past_len, Bd), jnp.arange(n_meta, dtype=F32)])
    tabs_b = _rope_tables(pos_b, rope_m, rt_dk)
    tabs_s = _rope_tables(pos_s, rope_m, rt_dk)

    lb_soft = jax.nn.softmax(hg_lb.astype(F32), axis=0)
    lb_all = jnp.cumsum(lb_soft, axis=0) - lb_soft[0]

    lane_pad = lambda g, lo: jnp.pad(g, (lo, LANES - lo - g.shape[0]))[None, :]
    sel = jnp.zeros((LANES, rope_m), BF16).at[jnp.arange(rope_m) + 64, jnp.arange(rope_m)].set(1.0)
    cache_kropet = jnp.swapaxes(cache_krope, 2, 3)

    outs = {k: [] for k in ("ckv_p", "kr_p", "hg_p", "rt_p", "ckv_s", "kr_s", "hg_s", "rt_s")}
    for l in range(depth):
        wi = w_in[l]
        o_kr = nq + nkv
        w_perm = jnp.concatenate(
            [wi[:, :o_kr], jnp.zeros((D, 64), F32), wi[:, o_kr:o_kr + rope_m], jnp.zeros((D, LANES - 64 - rope_m), F32),
             wi[:, o_kr + rope_m:]], axis=1).astype(BF16)
        wq = _slab_cols(w_uq[l], nope, rope_m).astype(BF16)
        wk = _slab_cols(w_uk[l], nope, rope_m).astype(BF16)
        wvt = w_uv[l].reshape(nkv, nh * vdim).T.astype(BF16)
        wukt = w_uk[l].reshape(nkv, nh * nope).T.astype(BF16)
        wabs_w = jnp.einsum('rhd,hg->hdgr', w_uk[l], jnp.eye(nh, dtype=F32))
        wabs_w = jnp.pad(wabs_w, ((0, 0), (0, LANES - nope), (0, 0), (0, 0))).reshape(nh * LANES, nh * nkv).astype(BF16)
        wuv_bd = jnp.einsum('rhd,hg->hrgd', w_uv[l], jnp.eye(nh, dtype=F32)).reshape(nh * nkv, nh * vdim).astype(BF16)
        g_q = (lane_pad(qn_nope_g[l], 0) + lane_pad(qn_rope_g[l], 64)).astype(F32)
        g_k = lane_pad(kn_nope_g[l], 0).astype(F32)
        g_kr = lane_pad(kn_rope_g[l], 64).astype(F32)
        g_hg = jnp.tile(hg_norm_g[l], LANES // hg_dk)[None, :]
        g_rt = jnp.tile(ret_norm_g[l], LANES // rt_dk)[None, :]
        wo = w_o[l].astype(BF16)
        inproj = functools.partial(_inproj, gn=norm_mix_g[l][None, :], w=w_perm, gq=q_lora_g[l][None, :],
                                   gkv=kv_lora_g[l][None, :], gkr=g_kr, lb=lb_all[l][None, :], wq=wq, gqs=g_q, gks=g_k,
                                   nq=nq, nkv=nkv, rope_m=rope_m, dh=dh, nh=nh, nope=nope, scale=scale)

        q_b, ckv_b, krs_b, hg_b, rt_b = inproj(h_b, tabs=tabs_b, with_qg=False)
        q_s, qg_s, ckv_s, krs_s, hg_s, rt_s = inproj(h_s, tabs=tabs_s, with_qg=True)
        k_b, vt_b = _kvup(ckv_b, krs_b, wk, wvt, g_k, nh=nh, nope=nope)
        k_m, vt_m = _kvup(ckv_s[ns:], krs_s[ns:], wk, wvt, g_k, nh=nh, nope=nope)

        o_b = _flash(q_b, k_b, vt_b, (k_m, vt_m), nh=nh, batch=B)
        mpad = LANES - n_meta
        o_m = _flash(jnp.pad(q_s[ns:], ((0, mpad), (0, 0))), jnp.pad(k_m, ((0, mpad), (0, 0))),
                     jnp.pad(vt_m, ((0, 0), (0, mpad))), None, nh=nh, batch=1)[:n_meta]
        wabs = _matmul(qg_s[:ns], wabs_w, BF16, "qabsorb").reshape(Bd, Sd * nh, nkv)
        pad_new = lambda a: jnp.pad(a.reshape(Bd, Sd, -1), ((0, 0), (0, LANES - Sd), (0, 0)))
        kr_s_compact = krs_s[:, 64:64 + rope_m]
        acc = _paged_attention(page_table, q_s[:ns].reshape(Bd, Sd * nh, LANES), wabs, wukt, sel,
                               pad_new(ckv_s[:ns]), pad_new(kr_s_compact[:ns]).swapaxes(1, 2), cache_ckv, cache_kropet, l,
                               nh=nh, nope=nope, sd=Sd)
        o_s = _matmul(acc.reshape(ns, nh * nkv), wuv_bd, F32, "vabsorb")
        o_small = jnp.concatenate([o_s, o_m], axis=0)

        zero_state = jnp.zeros((1, dh, dh), F32)
        rec = functools.partial(_recurrence, dh=dh, dk=hg_dk)
        ohg_m, sT_hg_m = rec("hgrn", hg_s[ns:][None], zero_state, chunk=SMALL_CHUNK, shared_state=True)
        ohg_s, sT_hg_s = rec("hgrn", hg_s[:ns].reshape(Bd, Sd, -1), _block_diag_t(state_hgrn[l].astype(F32)),
                             chunk=SMALL_CHUNK, shared_state=False)
        ohg_b, sT_hg_b = rec("hgrn", hg_b.reshape(B, S, -1), sT_hg_m, chunk=HG_CHUNK, shared_state=True)
        ort_m, sT_rt_m = rec("ret", rt_s[ns:][None], zero_state, chunk=SMALL_CHUNK, shared_state=True)
        ort_s, sT_rt_s = rec("ret", rt_s[:ns].reshape(Bd, Sd, -1), _block_diag_t(state_ret[l].astype(F32)),
                             chunk=SMALL_CHUNK, shared_state=False)
        ort_b, sT_rt_b = rec("ret", rt_b.reshape(B, S, -1), sT_rt_m, chunk=RET_CHUNK, shared_state=True)
        ohg_small = jnp.concatenate([ohg_s[:, :Sd].reshape(ns, dh), ohg_m[0, :n_meta]], axis=0)
        ort_small = jnp.concatenate([ort_s[:, :Sd].reshape(ns, dh), ort_m[0, :n_meta]], axis=0)

        merge = functools.partial(_merge, w_o=wo, gh=g_hg, gr=g_rt, dh=dh, dv=hg_dk)
        h_b = merge(o_b, ohg_b.reshape(B * S, dh), hg_b, ort_b.reshape(B * S, dh), rt_b, h_b)
        h_s = merge(o_small, ohg_small, hg_s, ort_small, rt_s, h_s)
        g_ffn = norm_ffn_g[l][None, :]
        i = l // 2
        if l % 2 == 0:
            mix = functools.partial(_ffn, g=g_ffn, wg=ffn_w_gate[i].astype(BF16), wu=ffn_w_up[i].astype(BF16),
                                    wd=ffn_w_down[i].astype(BF16))
        else:
            ne = moe_router.shape[2]
            wr = jnp.pad(moe_router[i], ((0, 0), (0, LANES - ne)))
            mix = functools.partial(_moe, g=g_ffn, wr=wr, wg=moe_w_gate[i].astype(BF16), wu=moe_w_up[i].astype(BF16),
                                    wd=moe_w_down[i].astype(BF16))
        h_b = mix(h_b)
        h_s = mix(h_s)

        with_meta = lambda real, meta: jnp.concatenate(
            [jnp.broadcast_to(meta[None], (B,) + meta.shape), real.reshape(B, S, -1)], axis=1)
        outs["ckv_p"].append(with_meta(ckv_b, ckv_s[ns:]))
        outs["kr_p"].append(with_meta(krs_b[:, 64:64 + rope_m], kr_s_compact[ns:]))
        outs["hg_p"].append(_unblock_t(sT_hg_b, hg_heads).astype(state_hgrn.dtype))
        outs["rt_p"].append(_unblock_t(sT_rt_b, rt_heads).astype(state_ret.dtype))
        outs["ckv_s"].append(ckv_s[:ns].reshape(Bd, Sd, nkv))
        outs["kr_s"].append(kr_s_compact[:ns].reshape(Bd, Sd, rope_m))
        outs["hg_s"].append(_unblock_t(sT_hg_s, hg_heads).astype(state_hgrn.dtype))
        outs["rt_s"].append(_unblock_t(sT_rt_s, rt_heads).astype(state_ret.dtype))

    y_prompt = h_b.reshape(B, S, D)
    y_sample = h_s[:ns].reshape(Bd, Sd, D)
    stack = lambda k: jnp.stack(outs[k])
    return (y_prompt, y_sample, stack("ckv_p"), stack("kr_p"), stack("hg_p"), stack("rt_p"),
            stack("ckv_s"), stack("kr_s"), stack("hg_s"), stack("rt_s"))
```

```python
import functools

import numpy as np
import jax
import jax.numpy as jnp
from jax import lax
from jax.experimental import pallas as pl
from jax.experimental.pallas import tpu as pltpu

F32 = jnp.float32
BF16 = jnp.bfloat16

EPS = 1e-6
ROPE_BASE = 10000.0
NEG_BIG = -1e30
LOG2E = 1.4426950408889634
LANES = 128
SUBLANES = 8
VMEM_LIMIT = 56 * 1024 * 1024

HG_CHUNK = 64
RET_CHUNK = 128
SMALL_CHUNK = 16
REC_BATCH_BLOCK = 8
PAGES_PER_STEP = 16
SCORE_BUFFERS = 3
FLASH_TILE = 1024


def _params(*sem):
    return pltpu.CompilerParams(dimension_semantics=sem, vmem_limit_bytes=VMEM_LIMIT)


def _row_tile(n, pref):
    if n <= pref:
        return n
    t = pref - pref % 16
    while t >= 16:
        if n % t == 0:
            return t
        t -= 16
    return n


def _dot(a, b):
    return jnp.dot(a.astype(BF16), b.astype(BF16), preferred_element_type=F32)


def _dot_nt(a, b):
    return lax.dot_general(a.astype(BF16), b.astype(BF16), (((1,), (1,)), ((), ())), preferred_element_type=F32)


def _dot_tn(a, b):
    return lax.dot_general(a.astype(BF16), b.astype(BF16), (((0,), (0,)), ((), ())), preferred_element_type=F32)


def _rms(x, width):
    return x * lax.rsqrt(jnp.sum(x * x, axis=-1, keepdims=True) * (1.0 / width) + EPS)


def _rope128(y, cos, sin_signed, half):
    lane = lax.broadcasted_iota(jnp.int32, y.shape, 1)
    first = (lane % (2 * half)) < half
    rot = jnp.where(first, pltpu.roll(y, LANES - half, 1), pltpu.roll(y, half, 1))
    return y * cos + rot * sin_signed


def _sigmoid(x):
    return 1.0 / (1.0 + jnp.exp(-x))


def _inproj_kernel(h_ref, gn_ref, w_ref, gq_ref, gkv_ref, gkr_ref, lb_ref, wq_ref, gqs_ref, gks_ref,
                   cm_ref, sm_ref, cr_ref, sr_ref, *out_refs, nq, nkv, rope_m, dh, nh, nope, scale):
    q_refs = out_refs[:-4]
    ckv_ref, kr_ref, hg_ref, rt_ref = out_refs[-4:]
    x = h_ref[...]
    xb = (_rms(x, x.shape[-1]) * gn_ref[...]).astype(BF16)

    def proj(lo, width):
        return jnp.dot(xb, w_ref[:, lo:lo + width], preferred_element_type=F32)

    cm = cm_ref[...]
    sm = sm_ref[...]
    c = proj(0, nq)
    cq = (_rms(c, nq) * gq_ref[...]).astype(BF16)
    gqs = gqs_ref[...]
    for h in range(nh):
        x = jnp.dot(cq, wq_ref[:, h * LANES:(h + 1) * LANES], preferred_element_type=F32)
        lane = lax.broadcasted_iota(jnp.int32, x.shape, 1)
        is_nope = lane < nope
        x2 = x * x
        ss_n = jnp.sum(jnp.where(is_nope, x2, 0.0), axis=-1, keepdims=True)
        ss_r = jnp.sum(jnp.where(is_nope, 0.0, x2), axis=-1, keepdims=True)
        inv = jnp.where(is_nope, lax.rsqrt(ss_n * (1.0 / nope) + EPS), lax.rsqrt(ss_r * (1.0 / rope_m) + EPS))
        y = _rope128(x * inv * gqs, cm, sm, rope_m // 2) * scale
        q_refs[0][:, h * LANES:(h + 1) * LANES] = y.astype(BF16)
        if len(q_refs) > 1:
            q_refs[1][:, h * LANES:(h + 1) * LANES] = (y * gks_ref[...]).astype(BF16)

    c = proj(nq, nkv)
    ckv_ref[...] = _rms(c, nkv) * gkv_ref[...]
    k = proj(nq + nkv, LANES)
    k = _rms(k, rope_m) * gkr_ref[...]
    kr_ref[...] = _rope128(k, cm, sm, rope_m // 2)

    base = nq + nkv + LANES
    hq = proj(base, dh)
    hg_ref[:, 0:dh] = hq * _sigmoid(hq)
    z = proj(base + dh, dh)
    lb = lb_ref[...]
    f = lb + (1.0 - lb) * _sigmoid(z)
    hg_ref[:, dh:2 * dh] = 1.0 - f
    hg_ref[:, 2 * dh:3 * dh] = jnp.log(f)
    hg_ref[:, 3 * dh:4 * dh] = proj(base + 2 * dh, dh)
    hg_ref[:, 4 * dh:5 * dh] = proj(base + 3 * dh, dh)

    base = base + 4 * dh
    cr = cr_ref[...]
    sr = sr_ref[...]
    for part, scale in ((0, 1.0), (1, 64 ** -0.5)):
        for s in range(dh // LANES):
            lo = part * dh + s * LANES
            y = _rope128(proj(base + lo, LANES), cr, sr, 32)
            rt_ref[:, lo:lo + LANES] = y * scale
    rt_ref[:, 2 * dh:3 * dh] = proj(base + 2 * dh, dh)
    rt_ref[:, 3 * dh:4 * dh] = proj(base + 3 * dh, dh)


def _inproj(h, gn, w, gq, gkv, gkr, lb, wq, gqs, gks, tabs, *, nq, nkv, rope_m, dh, nh, nope, scale, with_qg):
    n, d = h.shape
    tm = _row_tile(n, 512)
    cm, sm, cr, sr = tabs
    assert cm.shape[0] % tm == 0
    period = cm.shape[0] // tm
    row = lambda width: pl.BlockSpec((tm, width), lambda i: (i, 0))
    tab = pl.BlockSpec((tm, LANES), lambda i: (i % period, 0))
    const = lambda a: pl.BlockSpec(a.shape, lambda i: (0, 0))
    nqo = 2 if with_qg else 1
    return pl.pallas_call(
        functools.partial(_inproj_kernel, nq=nq, nkv=nkv, rope_m=rope_m, dh=dh, nh=nh, nope=nope, scale=scale),
        grid=(n // tm,),
        in_specs=[row(d), const(gn), const(w), const(gq), const(gkv), const(gkr), const(lb), const(wq), const(gqs),
                  const(gks), tab, tab, tab, tab],
        out_specs=[row(nh * LANES)] * nqo + [row(nkv), row(LANES), row(5 * dh), row(4 * dh)],
        out_shape=[jax.ShapeDtypeStruct((n, nh * LANES), BF16)] * nqo
                  + [jax.ShapeDtypeStruct((n, nkv), F32), jax.ShapeDtypeStruct((n, LANES), F32),
                     jax.ShapeDtypeStruct((n, 5 * dh), F32), jax.ShapeDtypeStruct((n, 4 * dh), F32)],
        compiler_params=_params("parallel"),
        name="inproj",
    )(h, gn, w, gq, gkv, gkr, lb, wq, gqs, gks, cm, sm, cr, sr)


def _kvup_kernel(c_ref, kr_ref, wk_ref, wvt_ref, gk_ref, k_ref, vt_ref, *, nh, nope):
    c = c_ref[...].astype(BF16)
    kr = kr_ref[...]
    gk = gk_ref[...]
    for h in range(nh):
        x = jnp.dot(c, wk_ref[:, h * LANES:(h + 1) * LANES], preferred_element_type=F32)
        k_ref[:, h * LANES:(h + 1) * LANES] = (_rms(x, nope) * gk + kr).astype(BF16)
    vt_ref[...] = _dot_nt(wvt_ref[...], c).astype(BF16)


def _kvup(ckv, krs, wk, wvt, gk, *, nh, nope):
    n, r = ckv.shape
    tm = _row_tile(n, 512)
    row = lambda width: pl.BlockSpec((tm, width), lambda i: (i, 0))
    const = lambda a: pl.BlockSpec(a.shape, lambda i: (0, 0))
    nv = wvt.shape[0]
    return pl.pallas_call(
        functools.partial(_kvup_kernel, nh=nh, nope=nope),
        grid=(n // tm,),
        in_specs=[row(r), row(LANES), const(wk), const(wvt), const(gk)],
        out_specs=[row(nh * LANES), pl.BlockSpec((nv, tm), lambda i: (0, i))],
        out_shape=[jax.ShapeDtypeStruct((n, nh * LANES), BF16), jax.ShapeDtypeStruct((nv, n), BF16)],
        compiler_params=_params("parallel"),
        name="kvup",
    )(ckv, krs, wk, wvt, gk)


def _flash_kernel(*refs, nh, vdim, has_prefix):
    if has_prefix:
        q_ref, k_ref, vt_ref, kp_ref, vtp_ref, o_ref, m_sc, l_sc, acc_sc, s_sc = refs
    else:
        q_ref, k_ref, vt_ref, o_ref, m_sc, l_sc, acc_sc, s_sc = refs
    qi = pl.program_id(1)
    ki = pl.program_id(2)

    def update(k_r, vt_r, masked):
        tk = k_r.shape[0]

        def scores(h):
            st = _dot_nt(k_r[:, h * LANES:(h + 1) * LANES], q_ref[:, h * LANES:(h + 1) * LANES])
            if masked:
                key = lax.broadcasted_iota(jnp.int32, st.shape, 0)
                qry = lax.broadcasted_iota(jnp.int32, st.shape, 1)
                st = jnp.where(key <= qry, st, NEG_BIG)
            s_sc[h % nbuf, 0:tk, :] = st

        nbuf = s_sc.shape[0]
        for h in range(nbuf - 1):
            scores(h)
        for h in range(nh):
            if h + nbuf - 1 < nh:
                scores(h + nbuf - 1)
            rows = slice(h * vdim, (h + 1) * vdim)
            st = s_sc[h % nbuf, 0:tk, :]
            m_prev = m_sc[h:h + 1, :]
            m_new = jnp.maximum(m_prev, jnp.max(st, axis=0, keepdims=True))
            alpha = jnp.exp2(m_prev - m_new)
            e = jnp.exp2(st - m_new)
            l_sc[h:h + 1, :] = alpha * l_sc[h:h + 1, :] + jnp.sum(e, axis=0, keepdims=True)
            m_sc[h:h + 1, :] = m_new
            pv = jnp.dot(vt_r[rows, :], e.astype(BF16), preferred_element_type=F32)
            acc_sc[rows, :] = acc_sc[rows, :] * alpha + pv

    @pl.when(ki == 0)
    def _():
        m_sc[...] = jnp.full(m_sc.shape, NEG_BIG, F32)
        l_sc[...] = jnp.zeros(l_sc.shape, F32)
        acc_sc[...] = jnp.zeros(acc_sc.shape, F32)
        if has_prefix:
            update(kp_ref, vtp_ref, False)

    @pl.when(ki < qi)
    def _():
        update(k_ref, vt_ref, False)

    @pl.when(ki == qi)
    def _():
        update(k_ref, vt_ref, True)
        for h in range(nh):
            rows = slice(h * vdim, (h + 1) * vdim)
            acc_sc[rows, :] = acc_sc[rows, :] / l_sc[h:h + 1, :]
        o_ref[...] = acc_sc[...].T


def _flash(q, k, vt, prefix, *, nh, batch):
    n = q.shape[0]
    t = n // batch
    dv = vt.shape[0]
    tq = _row_tile(t, FLASH_TILE)
    nt = t // tq
    qspec = pl.BlockSpec((tq, nh * LANES), lambda bi, qi, ki: (bi * nt + qi, 0))
    kspec = pl.BlockSpec((tq, nh * LANES), lambda bi, qi, ki: (bi * nt + jnp.minimum(ki, qi), 0))
    vspec = pl.BlockSpec((dv, tq), lambda bi, qi, ki: (0, bi * nt + jnp.minimum(ki, qi)))
    ins, specs = [q, k, vt], [qspec, kspec, vspec]
    if prefix is not None:
        for a in prefix:
            ins.append(a)
            specs.append(pl.BlockSpec(a.shape, lambda bi, qi, ki: (0, 0)))
    return pl.pallas_call(
        functools.partial(_flash_kernel, nh=nh, vdim=dv // nh, has_prefix=prefix is not None),
        grid=(batch, nt, nt),
        in_specs=specs,
        out_specs=pl.BlockSpec((tq, dv), lambda bi, qi, ki: (bi * nt + qi, 0)),
        out_shape=jax.ShapeDtypeStruct((n, dv), F32),
        scratch_shapes=[pltpu.VMEM((nh, tq), F32), pltpu.VMEM((nh, tq), F32), pltpu.VMEM((dv, tq), F32),
                        pltpu.VMEM((SCORE_BUFFERS, tq, tq), F32)],
        compiler_params=_params("parallel", "parallel", "arbitrary"),
        name="flash_prefix" if prefix is not None else "flash",
    )(*ins)


def _matmul_kernel(a_ref, w_ref, o_ref):
    o_ref[...] = jnp.dot(a_ref[...].astype(BF16), w_ref[...], preferred_element_type=F32).astype(o_ref.dtype)


def _matmul(a, w, out_dtype, name):
    n, k = a.shape
    tm = _row_tile(n, 512)
    return pl.pallas_call(
        _matmul_kernel,
        grid=(n // tm,),
        in_specs=[pl.BlockSpec((tm, k), lambda i: (i, 0)), pl.BlockSpec(w.shape, lambda i: (0, 0))],
        out_specs=pl.BlockSpec((tm, w.shape[1]), lambda i: (i, 0)),
        out_shape=jax.ShapeDtypeStruct((n, w.shape[1]), out_dtype),
        compiler_params=_params("parallel"),
        name=name,
    )(a, w)


def _paged_kernel(pt_ref, q_ref, wabs_ref, wukt_ref, sel_ref, cn_ref, krn_ref, ckv_hbm, krt_hbm, o_ref,
                  lhs_sc, cbuf, kbuf, sem, *, layer, npg, nchunk, sub, nh, nope, sd):
    b = pl.program_id(0)
    nb = pl.num_programs(0)
    nk = nh * nope
    rows = sd * nh
    page = ckv_hbm.shape[2]
    nslot = cbuf.shape[0]

    def chunk_slot(off):
        return lax.rem(b * nchunk + off, nslot)

    def page_copies(off, p):
        slot = chunk_slot(off)
        idx = pt_ref[b + off // nchunk, (off % nchunk) * npg + p]
        span = pl.ds(p * page, page)
        return (pltpu.make_async_copy(ckv_hbm.at[layer, idx], cbuf.at[slot, span, :], sem.at[0, slot]),
                pltpu.make_async_copy(krt_hbm.at[layer, idx], kbuf.at[slot, :, span], sem.at[1, slot]))

    def start_chunk(off):
        for p in range(npg):
            for cp in page_copies(off, p):
                cp.start()

    def wait_chunk(off):
        for p in range(npg):
            for cp in page_copies(off, p):
                cp.wait()

    @pl.when(b == 0)
    def _():
        start_chunk(0)
        start_chunk(1)

    lhs_sc[0:nk, :] = wukt_ref[...]
    lhs_sc[nk:nk + rows, :] = wabs_ref[0]
    qr = jnp.dot(q_ref[0], sel_ref[...], preferred_element_type=F32).astype(BF16)

    def scores(c, krt):
        big = _dot_nt(lhs_sc[...], c)
        kraw = big[0:nk]
        ssq = jnp.sum((kraw * kraw).reshape(nh, nope, kraw.shape[-1]), axis=1)
        r = lax.rsqrt(ssq * (1.0 / nope) + EPS)
        return big[nk:nk + rows] * jnp.concatenate([r] * sd, axis=0) + jnp.dot(qr, krt, preferred_element_type=F32)

    def update(state, s, c):
        m_prev, l_prev, acc = state
        m_new = jnp.maximum(m_prev, jnp.max(s, axis=-1, keepdims=True))
        alpha = jnp.exp2(m_prev - m_new)
        e = jnp.exp2(s - m_new)
        l_new = alpha * l_prev + jnp.sum(e, axis=-1, keepdims=True)
        return m_new, l_new, acc * alpha + jnp.dot(e.astype(BF16), c, preferred_element_type=F32)

    def chunk_scores(off):
        slot = chunk_slot(off)
        span = sub * page
        cs, ss = [], []
        for g in range(npg // sub):
            lat = cbuf[slot, g * span:(g + 1) * span, :].astype(BF16)
            cs.append(lat)
            ss.append(scores(lat, kbuf[slot, :, g * span:(g + 1) * span].astype(BF16)))
        return jnp.concatenate(ss, axis=1), jnp.concatenate(cs, axis=0)

    state = (jnp.full((rows, 1), NEG_BIG, F32), jnp.zeros((rows, 1), F32), jnp.zeros((rows, cbuf.shape[2]), F32))
    wait_chunk(0)
    cur = chunk_scores(0)
    for c in range(nchunk):
        if c + 2 < nchunk:
            start_chunk(c + 2)
        else:
            @pl.when(b + 1 < nb)
            def _():
                start_chunk(c + 2)
        if c + 1 < nchunk:
            wait_chunk(c + 1)
            nxt = chunk_scores(c + 1)
        state = update(state, *cur)
        if c + 1 < nchunk:
            cur = nxt

    lat = cn_ref[0].astype(BF16)
    s = scores(lat, krn_ref[0].astype(BF16))
    row = lax.broadcasted_iota(jnp.int32, s.shape, 0)
    col = lax.broadcasted_iota(jnp.int32, s.shape, 1)
    _, l_fin, acc = update(state, jnp.where(col * nh <= row, s, NEG_BIG), lat)
    o_ref[0] = acc / l_fin


def _paged_attention(page_table, q, wabs, wukt, sel, cn, krnt, cache_ckv, cache_kropet, layer, *, nh, nope, sd):
    bd, rows, _ = q.shape
    n_pages = page_table.shape[1]
    assert n_pages % 2 == 0
    npg = max(p for p in range(1, PAGES_PER_STEP + 1) if n_pages % (2 * p) == 0)
    nchunk = n_pages // npg
    nslot = 3
    sub = 2 if npg % 2 == 0 else 1
    page, kvl = cache_ckv.shape[2:]
    rp = cache_kropet.shape[2]
    per_b = lambda a: pl.BlockSpec((1,) + a.shape[1:], lambda b, pt: (b, 0, 0))
    const = lambda a: pl.BlockSpec(a.shape, lambda b, pt: (0, 0))
    hbm = pl.BlockSpec(memory_space=pl.ANY)
    grid_spec = pltpu.PrefetchScalarGridSpec(
        num_scalar_prefetch=1,
        grid=(bd,),
        in_specs=[per_b(q), per_b(wabs), const(wukt), const(sel), per_b(cn), per_b(krnt), hbm, hbm],
        out_specs=pl.BlockSpec((1, rows, kvl), lambda b, pt: (b, 0, 0)),
        scratch_shapes=[pltpu.VMEM((nh * nope + rows, kvl), BF16), pltpu.VMEM((nslot, npg * page, kvl), F32),
                        pltpu.VMEM((nslot, rp, npg * page), F32), pltpu.SemaphoreType.DMA((2, nslot))],
    )
    return pl.pallas_call(
        functools.partial(_paged_kernel, layer=layer, npg=npg, nchunk=nchunk, sub=sub, nh=nh, nope=nope, sd=sd),
        grid_spec=grid_spec,
        out_shape=jax.ShapeDtypeStruct((bd, rows, kvl), F32),
        compiler_params=_params("arbitrary"),
        name="paged_attention",
    )(page_table, q, wabs, wukt, sel, cn, krnt, cache_ckv, cache_kropet)


def _block_mask(n, blk):
    r = lax.broadcasted_iota(jnp.int32, (n, n), 0) // blk
    c = lax.broadcasted_iota(jnp.int32, (n, n), 1) // blk
    return r == c


def _expand_state(c, dh, dk):
    return jnp.where(_block_mask(dh, dk), jnp.concatenate([c] * (dh // dk), axis=0), 0.0)


def _collapse_state(st, dk):
    out = st[0:dk, :]
    for h in range(1, st.shape[0] // dk):
        out = out + st[h * dk:(h + 1) * dk, :]
    return out


def _split3(x):
    hi = x.astype(BF16)
    r1 = x - hi.astype(F32)
    mid = r1.astype(BF16)
    lo = (r1 - mid.astype(F32)).astype(BF16)
    return hi, mid, lo


def _hgrn_kernel(x_ref, s0_ref, o_ref, sT_ref, st_sc, pad_sc, sh_sc, e_sc, *, dh, dk, chunk, bb):
    for i in range(bb):
        _hgrn_one(x_ref.at[i], s0_ref.at[i % s0_ref.shape[0]], o_ref.at[i], sT_ref.at[i],
                  st_sc.at[i], pad_sc.at[i], sh_sc.at[i], e_sc.at[i], dh=dh, dk=dk, chunk=chunk)


def _hgrn_one(x_ref, s0_ref, o_ref, sT_ref, st_sc, pad_sc, sh_sc, e_sc, *, dh, dk, chunk):
    ci = pl.program_id(1)

    @pl.when(ci == 0)
    def _():
        st_sc[...] = _expand_state(s0_ref[...], dh, dk)
        pad_sc[:, 0:SUBLANES, :] = jnp.zeros((3, SUBLANES, dh), F32)

    q = x_ref[:, 0:dh]
    k = x_ref[:, dh:2 * dh]
    g = x_ref[:, 2 * dh:3 * dh]
    v = x_ref[:, 3 * dh:4 * dh]
    r = lax.broadcasted_iota(jnp.int32, (chunk, chunk), 0)
    c = lax.broadcasted_iota(jnp.int32, (chunk, chunk), 1)
    tri = jnp.where(c <= r, 1.0, 0.0).astype(BF16)
    b = sum(jnp.dot(tri, part, preferred_element_type=F32) for part in _split3(g * LOG2E))
    for i, val in enumerate((k, b, v)):
        pad_sc[i, SUBLANES:SUBLANES + chunk, :] = val
        sh_sc[i, 0] = val
        for r in range(1, SUBLANES):
            sh_sc[i, r] = pad_sc[i, SUBLANES - r:SUBLANES - r + chunk, :]
    ones_bd = jnp.where(_block_mask(dh, dk), 1.0, 0.0).astype(BF16)

    o_ref[...] = _dot_nt(q * jnp.exp2(b), st_sc[...])
    bands = []
    off = 0
    for a in range(chunk // SUBLANES):
        lo = a * SUBLANES
        rows = chunk - lo
        qa = q[lo:, :]
        ba = b[lo:, :]
        for r in range(SUBLANES):
            e_sc[off:off + rows, :] = qa * sh_sc[0, r, 0:rows, :] * jnp.exp2(ba - sh_sc[1, r, 0:rows, :])
            bands.append((lo, rows, r, off))
            off += rows
    e_sc[...] = jnp.dot(e_sc[...].astype(BF16), ones_bd, preferred_element_type=F32)
    for a in range(chunk // SUBLANES):
        acc = None
        for lo, rows, r, off in bands[a * SUBLANES:(a + 1) * SUBLANES]:
            term = e_sc[off:off + rows, :] * sh_sc[2, r, 0:rows, :]
            acc = term if acc is None else acc + term
        o_ref[lo:, :] += acc

    b_last = b[chunk - 1:chunk, :]
    upd = _dot_tn(v, k * jnp.exp2(b_last - b))
    st_sc[...] = st_sc[...] * jnp.exp2(b_last) + jnp.where(_block_mask(dh, dk), upd, 0.0)

    @pl.when(ci == pl.num_programs(1) - 1)
    def _():
        sT_ref[...] = _collapse_state(st_sc[...], dk)


def _ret_kernel(x_ref, s0_ref, o_ref, sT_ref, st_sc, *, bb, **kw):
    for i in range(bb):
        _ret_one(x_ref.at[i], s0_ref.at[i % s0_ref.shape[0]], o_ref.at[i], sT_ref.at[i], st_sc.at[i], **kw)


def _ret_one(x_ref, s0_ref, o_ref, sT_ref, st_sc, *, dh, dk, chunk, n_valid, log_gamma):
    ci = pl.program_id(1)

    @pl.when(ci == 0)
    def _():
        st_sc[...] = _expand_state(s0_ref[...], dh, dk)

    q = x_ref[:, 0:dh]
    k = x_ref[:, dh:2 * dh]
    v = x_ref[:, 2 * dh:3 * dh]
    lane = lax.broadcasted_iota(jnp.int32, (chunk, dh), 1)
    head = lane // dk
    lg = jnp.zeros((chunk, dh), F32)
    for h, val in enumerate(log_gamma):
        lg = jnp.where(head == h, val, lg)
    t = lax.broadcasted_iota(jnp.int32, (chunk, dh), 0).astype(F32)
    r = lax.broadcasted_iota(jnp.int32, (chunk, chunk), 0)
    c = lax.broadcasted_iota(jnp.int32, (chunk, chunk), 1)
    diff = (r - c).astype(F32)

    o = _dot_nt(q, st_sc[...]) * jnp.exp((t + 1.0) * lg)
    for h, val in enumerate(log_gamma):
        in_h = head == h
        a = _dot_nt(jnp.where(in_h, q, 0.0), k)
        a = a * jnp.where(diff >= 0, jnp.exp(jnp.maximum(diff, 0.0) * val), 0.0)
        o = o + jnp.where(in_h, _dot(a, v), 0.0)
    o_ref[...] = o

    upd = _dot_tn(v, k * jnp.exp((n_valid - 1.0 - t) * lg))
    st_sc[...] = st_sc[...] * jnp.exp(n_valid * lg[0:1, :]) + jnp.where(_block_mask(dh, dk), upd, 0.0)

    @pl.when(ci == pl.num_programs(1) - 1)
    def _():
        sT_ref[...] = _collapse_state(st_sc[...], dk)


def _recurrence(kind, x, s0, *, dh, dk, chunk, shared_state):
    b, t, w = x.shape
    n_valid = min(chunk, t)
    if t < chunk:
        x = jnp.pad(x, ((0, 0), (0, chunk - t), (0, 0)))
    t_pad = x.shape[1]
    bb = max(d for d in range(1, REC_BATCH_BLOCK + 1) if b % d == 0)
    s0_spec = (pl.BlockSpec((1, dk, dh), lambda bi, ci: (0, 0, 0)) if shared_state
               else pl.BlockSpec((bb, dk, dh), lambda bi, ci: (bi, 0, 0)))
    if kind == "hgrn":
        body = functools.partial(_hgrn_kernel, dh=dh, dk=dk, chunk=chunk, bb=bb)
        band_rows = SUBLANES * sum(range(SUBLANES, chunk + 1, SUBLANES))
        scratch = [pltpu.VMEM((bb, dh, dh), F32), pltpu.VMEM((bb, 3, SUBLANES + chunk, dh), F32),
                   pltpu.VMEM((bb, 3, SUBLANES, chunk, dh), F32), pltpu.VMEM((bb, band_rows, dh), F32)]
    else:
        log_gamma = tuple(float(np.log1p(-np.exp2(-5.0 - h))) for h in range(dh // dk))
        body = functools.partial(_ret_kernel, bb=bb, dh=dh, dk=dk, chunk=chunk, n_valid=n_valid,
                                 log_gamma=log_gamma)
        scratch = [pltpu.VMEM((bb, dh, dh), F32)]
    return pl.pallas_call(
        body,
        grid=(b // bb, t_pad // chunk),
        in_specs=[pl.BlockSpec((bb, chunk, w), lambda bi, ci: (bi, ci, 0)), s0_spec],
        out_specs=[pl.BlockSpec((bb, chunk, dh), lambda bi, ci: (bi, ci, 0)),
                   pl.BlockSpec((bb, dk, dh), lambda bi, ci: (bi, 0, 0))],
        out_shape=[jax.ShapeDtypeStruct((b, t_pad, dh), F32), jax.ShapeDtypeStruct((b, dk, dh), F32)],
        scratch_shapes=scratch,
        compiler_params=_params("parallel", "arbitrary"),
        name=kind,
    )(x, s0)


def _merge_kernel(om_ref, oh_ref, hg_ref, or_ref, rg_ref, h_ref, w_ref, gh_ref, gr_ref, o_ref, *, nm, dh, dv):
    def gated(o_r, gate_r, g_r):
        o = o_r[...]
        gate = gate_r[...]
        out = []
        for s in range(dh // LANES):
            x = o[:, s * LANES:(s + 1) * LANES]
            lane = lax.broadcasted_iota(jnp.int32, x.shape, 1)
            low = lane < dv
            x2 = x * x
            ss_l = jnp.sum(jnp.where(low, x2, 0.0), axis=-1, keepdims=True)
            ss_h = jnp.sum(jnp.where(low, 0.0, x2), axis=-1, keepdims=True)
            inv = lax.rsqrt(jnp.where(low, ss_l, ss_h) * (1.0 / dv) + EPS)
            gt = gate[:, s * LANES:(s + 1) * LANES]
            out.append(x * inv * g_r[...] * (gt * _sigmoid(gt)))
        return jnp.concatenate(out, axis=-1)

    y = _dot(om_ref[...], w_ref[0:nm, :])
    y = y + _dot(gated(oh_ref, hg_ref, gh_ref), w_ref[nm:nm + dh, :])
    y = y + _dot(gated(or_ref, rg_ref, gr_ref), w_ref[nm + dh:nm + 2 * dh, :])
    o_ref[...] = h_ref[...] + y


def _merge(o_mla, o_hg, hg_pack, o_rt, rt_pack, h, w_o, gh, gr, *, dh, dv):
    n, d = h.shape
    nm = o_mla.shape[1]
    tm = _row_tile(n, 512)
    row = lambda width, cb=0: pl.BlockSpec((tm, width), lambda i: (i, cb))
    const = lambda a: pl.BlockSpec(a.shape, lambda i: (0, 0))
    return pl.pallas_call(
        functools.partial(_merge_kernel, nm=nm, dh=dh, dv=dv),
        grid=(n // tm,),
        in_specs=[row(nm), row(dh), row(dh, 4), row(dh), row(dh, 3), row(d), const(w_o), const(gh), const(gr)],
        out_specs=row(d),
        out_shape=jax.ShapeDtypeStruct((n, d), F32),
        compiler_params=_params("parallel"),
        name="merge",
    )(o_mla, o_hg, hg_pack, o_rt, rt_pack, h, w_o, gh, gr)


def _ffn_kernel(h_ref, g_ref, wg_ref, wu_ref, wd_ref, o_ref, xn_sc, acc_sc):
    j = pl.program_id(1)

    @pl.when(j == 0)
    def _():
        x = h_ref[...]
        xn_sc[...] = (_rms(x, x.shape[-1]) * g_ref[...]).astype(BF16)
        acc_sc[...] = jnp.zeros(acc_sc.shape, F32)

    xn = xn_sc[...]
    a = jnp.dot(xn, wg_ref[...], preferred_element_type=F32)
    u = jnp.dot(xn, wu_ref[...], preferred_element_type=F32)
    acc_sc[...] += jnp.dot((a * _sigmoid(a) * u).astype(BF16), wd_ref[...], preferred_element_type=F32)

    @pl.when(j == pl.num_programs(1) - 1)
    def _():
        o_ref[...] = h_ref[...] + acc_sc[...]


def _ffn(h, g, wg, wu, wd):
    n, d = h.shape
    ff = wg.shape[1]
    tm = _row_tile(n, 512)
    tf = ff // 2 if (ff // 2) % LANES == 0 else ff
    return pl.pallas_call(
        _ffn_kernel,
        grid=(n // tm, ff // tf),
        in_specs=[pl.BlockSpec((tm, d), lambda i, j: (i, 0)), pl.BlockSpec(g.shape, lambda i, j: (0, 0)),
                  pl.BlockSpec((d, tf), lambda i, j: (0, j)), pl.BlockSpec((d, tf), lambda i, j: (0, j)),
                  pl.BlockSpec((tf, d), lambda i, j: (j, 0))],
        out_specs=pl.BlockSpec((tm, d), lambda i, j: (i, 0)),
        out_shape=jax.ShapeDtypeStruct((n, d), F32),
        scratch_shapes=[pltpu.VMEM((tm, d), BF16), pltpu.VMEM((tm, d), F32)],
        compiler_params=_params("parallel", "arbitrary"),
        name="ffn",
    )(h, g, wg, wu, wd)


def _moe_kernel(h_ref, g_ref, wr_ref, wg_ref, wu_ref, wd_ref, o_ref, xn_sc, gate_sc, acc_sc, *, ne):
    e = pl.program_id(1)

    @pl.when(e == 0)
    def _():
        x = h_ref[...]
        xn = _rms(x, x.shape[-1]) * g_ref[...]
        xn_sc[...] = xn.astype(BF16)
        acc_sc[...] = jnp.zeros(acc_sc.shape, F32)
        xh, xm, _ = _split3(xn)
        wh, wm, _ = _split3(wr_ref[...])
        dd = lambda a, b: jnp.dot(a, b, preferred_element_type=F32)
        logits = dd(xh, wh) + (dd(xh, wm) + dd(xm, wh))
        lane = lax.broadcasted_iota(jnp.int32, logits.shape, 1)
        valid = lane < ne
        lg = jnp.where(valid, logits, -jnp.inf)
        v1 = jnp.max(lg, axis=-1, keepdims=True)
        i1 = jnp.min(jnp.where(lg == v1, lane, LANES), axis=-1, keepdims=True)
        lg2 = jnp.where(lane == i1, -jnp.inf, lg)
        v2 = jnp.max(lg2, axis=-1, keepdims=True)
        i2 = jnp.min(jnp.where(lg2 == v2, lane, LANES), axis=-1, keepdims=True)
        e2 = jnp.exp(v2 - v1)
        w1 = 1.0 / (1.0 + e2)
        w2 = e2 / (1.0 + e2)
        gate_sc[...] = jnp.where(lane == i1, w1, jnp.where(lane == i2, w2, 0.0))

    lane = lax.broadcasted_iota(jnp.int32, gate_sc.shape, 1)
    gate = jnp.sum(jnp.where(lane == e, gate_sc[...], 0.0), axis=-1, keepdims=True)
    xn = xn_sc[...]
    a = jnp.dot(xn, wg_ref[0], preferred_element_type=F32)
    u = jnp.dot(xn, wu_ref[0], preferred_element_type=F32)
    acc_sc[...] += gate * jnp.dot((a * _sigmoid(a) * u).astype(BF16), wd_ref[0], preferred_element_type=F32)

    @pl.when(e == ne - 1)
    def _():
        o_ref[...] = h_ref[...] + acc_sc[...]


def _moe(h, g, wr, wg, wu, wd):
    n, d = h.shape
    ne, _, ff = wg.shape
    tm = _row_tile(n, 512)
    return pl.pallas_call(
        functools.partial(_moe_kernel, ne=ne),
        grid=(n // tm, ne),
        in_specs=[pl.BlockSpec((tm, d), lambda i, e: (i, 0)), pl.BlockSpec(g.shape, lambda i, e: (0, 0)),
                  pl.BlockSpec(wr.shape, lambda i, e: (0, 0)),
                  pl.BlockSpec((1, d, ff), lambda i, e: (e, 0, 0)), pl.BlockSpec((1, d, ff), lambda i, e: (e, 0, 0)),
                  pl.BlockSpec((1, ff, d), lambda i, e: (e, 0, 0))],
        out_specs=pl.BlockSpec((tm, d), lambda i, e: (i, 0)),
        out_shape=jax.ShapeDtypeStruct((n, d), F32),
        scratch_shapes=[pltpu.VMEM((tm, d), BF16), pltpu.VMEM((tm, LANES), F32), pltpu.VMEM((tm, d), F32)],
        compiler_params=_params("parallel", "arbitrary"),
        name="moe",
    )(h, g, wr, wg, wu, wd)


def _rope_tables(pos, rope_m, ret_dk):
    def angles(d):
        inv = ROPE_BASE ** (-jnp.arange(0, d, 2, dtype=F32) / d)
        return pos.astype(F32)[:, None] * inv[None, :]

    n = pos.shape[0]
    a = angles(rope_m)
    c, s = jnp.cos(a), jnp.sin(a)
    pad = LANES - 64 - rope_m
    cm = jnp.concatenate([jnp.ones((n, 64), F32), c, c, jnp.ones((n, pad), F32)], axis=1)
    sm = jnp.concatenate([jnp.zeros((n, 64), F32), -s, s, jnp.zeros((n, pad), F32)], axis=1)
    a = angles(ret_dk)
    c, s = jnp.cos(a), jnp.sin(a)
    reps = LANES // ret_dk
    cr = jnp.concatenate([c, c] * reps, axis=1)
    sr = jnp.concatenate([-s, s] * reps, axis=1)
    return cm, sm, cr, sr


def _slab_cols(w, nope, rope_m):
    r, nh, dd = w.shape
    return jnp.pad(w, ((0, 0), (0, 0), (0, LANES - dd))).reshape(r, nh * LANES)


def _block_diag_t(s):
    b, nh, dk, dv = s.shape
    return s.transpose(0, 3, 1, 2).reshape(b, dv, nh * dk)


def _unblock_t(st, nh):
    b, dv, n = st.shape
    return st.reshape(b, dv, nh, n // nh).transpose(0, 2, 3, 1)


def kernel(x_prompt, x_sample, cache_ckv, cache_krope, page_table, state_hgrn, state_ret, meta_tokens,
           norm_mix_g, w_in, q_lora_g, kv_lora_g, w_uq, w_uk, w_uv, qn_nope_g, qn_rope_g, kn_nope_g, kn_rope_g,
           hg_lb, hg_norm_g, ret_norm_g, w_o, norm_ffn_g, ffn_w_gate, ffn_w_up, ffn_w_down,
           moe_router, moe_w_gate, moe_w_up, moe_w_down):
    depth = w_in.shape[0]
    B, S, D = x_prompt.shape
    Bd, Sd, _ = x_sample.shape
    n_meta = meta_tokens.shape[0]
    nq, nkv = q_lora_g.shape[1], kv_lora_g.shape[1]
    nh, nope, rope_m = w_uq.shape[2], qn_nope_g.shape[1], qn_rope_g.shape[1]
    vdim = w_uv.shape[3]
    hg_heads, hg_dk = state_hgrn.shape[2], state_hgrn.shape[3]
    rt_heads, rt_dk = state_ret.shape[2], state_ret.shape[3]
    dh = hg_heads * hg_dk
    assert nope == 64 and rope_m == 32 and vdim == 64 and hg_dk == 64 and rt_dk == 64
    assert dh == rt_heads * rt_dk and nh % 2 == 0
    past_len = page_table.shape[1] * cache_ckv.shape[2]
    scale = float((nope + rope_m) ** -0.5 * LOG2E)
    ns = Bd * Sd

    h_b = x_prompt.reshape(B * S, D)
    h_s = jnp.concatenate([x_sample.reshape(ns, D), meta_tokens.astype(x_prompt.dtype)], axis=0)
    pos_b = jnp.arange(S, dtype=F32) + n_meta
    pos_s = jnp.concatenate([jnp.tile(jnp.arange(Sd, dtype=F32) + past_len, Bd), jnp.arange(n_meta, dtype=F32)])
    tabs_b = _rope_tables(pos_b, rope_m, rt_dk)
    tabs_s = _rope_tables(pos_s, rope_m, rt_dk)

    lb_soft = jax.nn.softmax(hg_lb.astype(F32), axis=0)
    lb_all = jnp.cumsum(lb_soft, axis=0) - lb_soft[0]

    lane_pad = lambda g, lo: jnp.pad(g, (lo, LANES - lo - g.shape[0]))[None, :]
    sel = jnp.zeros((LANES, rope_m), BF16).at[jnp.arange(rope_m) + 64, jnp.arange(rope_m)].set(1.0)
    cache_kropet = jnp.swapaxes(cache_krope, 2, 3)

    outs = {k: [] for k in ("ckv_p", "kr_p", "hg_p", "rt_p", "ckv_s", "kr_s", "hg_s", "rt_s")}
    for l in range(depth):
        wi = w_in[l]
        o_kr = nq + nkv
        w_perm = jnp.concatenate(
            [wi[:, :o_kr], jnp.zeros((D, 64), F32), wi[:, o_kr:o_kr + rope_m], jnp.zeros((D, LANES - 64 - rope_m), F32),
             wi[:, o_kr + rope_m:]], axis=1).astype(BF16)
        wq = _slab_cols(w_uq[l], nope, rope_m).astype(BF16)
        wk = _slab_cols(w_uk[l], nope, rope_m).astype(BF16)
        wvt = w_uv[l].reshape(nkv, nh * vdim).T.astype(BF16)
        wukt = w_uk[l].reshape(nkv, nh * nope).T.astype(BF16)
        wabs_w = jnp.einsum('rhd,hg->hdgr', w_uk[l], jnp.eye(nh, dtype=F32))
        wabs_w = jnp.pad(wabs_w, ((0, 0), (0, LANES - nope), (0, 0), (0, 0))).reshape(nh * LANES, nh * nkv).astype(BF16)
        wuv_bd = jnp.einsum('rhd,hg->hrgd', w_uv[l], jnp.eye(nh, dtype=F32)).reshape(nh * nkv, nh * vdim).astype(BF16)
        g_q = (lane_pad(qn_nope_g[l], 0) + lane_pad(qn_rope_g[l], 64)).astype(F32)
        g_k = lane_pad(kn_nope_g[l], 0).astype(F32)
        g_kr = lane_pad(kn_rope_g[l], 64).astype(F32)
        g_hg = jnp.tile(hg_norm_g[l], LANES // hg_dk)[None, :]
        g_rt = jnp.tile(ret_norm_g[l], LANES // rt_dk)[None, :]
        wo = w_o[l].astype(BF16)
        inproj = functools.partial(_inproj, gn=norm_mix_g[l][None, :], w=w_perm, gq=q_lora_g[l][None, :],
                                   gkv=kv_lora_g[l][None, :], gkr=g_kr, lb=lb_all[l][None, :], wq=wq, gqs=g_q, gks=g_k,
                                   nq=nq, nkv=nkv, rope_m=rope_m, dh=dh, nh=nh, nope=nope, scale=scale)

        q_b, ckv_b, krs_b, hg_b, rt_b = inproj(h_b, tabs=tabs_b, with_qg=False)
        q_s, qg_s, ckv_s, krs_s, hg_s, rt_s = inproj(h_s, tabs=tabs_s, with_qg=True)
        k_b, vt_b = _kvup(ckv_b, krs_b, wk, wvt, g_k, nh=nh, nope=nope)
        k_m, vt_m = _kvup(ckv_s[ns:], krs_s[ns:], wk, wvt, g_k, nh=nh, nope=nope)

        o_b = _flash(q_b, k_b, vt_b, (k_m, vt_m), nh=nh, batch=B)
        mpad = LANES - n_meta
        o_m = _flash(jnp.pad(q_s[ns:], ((0, mpad), (0, 0))), jnp.pad(k_m, ((0, mpad), (0, 0))),
                     jnp.pad(vt_m, ((0, 0), (0, mpad))), None, nh=nh, batch=1)[:n_meta]
        wabs = _matmul(qg_s[:ns], wabs_w, BF16, "qabsorb").reshape(Bd, Sd * nh, nkv)
        pad_new = lambda a: jnp.pad(a.reshape(Bd, Sd, -1), ((0, 0), (0, LANES - Sd), (0, 0)))
        kr_s_compact = krs_s[:, 64:64 + rope_m]
        acc = _paged_attention(page_table, q_s[:ns].reshape(Bd, Sd * nh, LANES), wabs, wukt, sel,
                               pad_new(ckv_s[:ns]), pad_new(kr_s_compact[:ns]).swapaxes(1, 2), cache_ckv, cache_kropet, l,
                               nh=nh, nope=nope, sd=Sd)
        o_s = _matmul(acc.reshape(ns, nh * nkv), wuv_bd, F32, "vabsorb")
        o_small = jnp.concatenate([o_s, o_m], axis=0)

        zero_state = jnp.zeros((1, hg_dk, dh), F32)
        rec = functools.partial(_recurrence, dh=dh, dk=hg_dk)
        ohg_m, sT_hg_m = rec("hgrn", hg_s[ns:][None], zero_state, chunk=SMALL_CHUNK, shared_state=True)
        ohg_s, sT_hg_s = rec("hgrn", hg_s[:ns].reshape(Bd, Sd, -1), _block_diag_t(state_hgrn[l].astype(F32)),
                             chunk=SMALL_CHUNK, shared_state=False)
        ohg_b, sT_hg_b = rec("hgrn", hg_b.reshape(B, S, -1), sT_hg_m, chunk=HG_CHUNK, shared_state=True)
        ort_m, sT_rt_m = rec("ret", rt_s[ns:][None], zero_state, chunk=SMALL_CHUNK, shared_state=True)
        ort_s, sT_rt_s = rec("ret", rt_s[:ns].reshape(Bd, Sd, -1), _block_diag_t(state_ret[l].astype(F32)),
                             chunk=SMALL_CHUNK, shared_state=False)
        ort_b, sT_rt_b = rec("ret", rt_b.reshape(B, S, -1), sT_rt_m, chunk=RET_CHUNK, shared_state=True)
        ohg_small = jnp.concatenate([ohg_s[:, :Sd].reshape(ns, dh), ohg_m[0, :n_meta]], axis=0)
        ort_small = jnp.concatenate([ort_s[:, :Sd].reshape(ns, dh), ort_m[0, :n_meta]], axis=0)

        merge = functools.partial(_merge, w_o=wo, gh=g_hg, gr=g_rt, dh=dh, dv=hg_dk)
        h_b = merge(o_b, ohg_b.reshape(B * S, dh), hg_b, ort_b.reshape(B * S, dh), rt_b, h_b)
        h_s = merge(o_small, ohg_small, hg_s, ort_small, rt_s, h_s)
        g_ffn = norm_ffn_g[l][None, :]
        i = l // 2
        if l % 2 == 0:
            mix = functools.partial(_ffn, g=g_ffn, wg=ffn_w_gate[i].astype(BF16), wu=ffn_w_up[i].astype(BF16),
                                    wd=ffn_w_down[i].astype(BF16))
        else:
            ne = moe_router.shape[2]
            wr = jnp.pad(moe_router[i], ((0, 0), (0, LANES - ne)))
            mix = functools.partial(_moe, g=g_ffn, wr=wr, wg=moe_w_gate[i].astype(BF16), wu=moe_w_up[i].astype(BF16),
                                    wd=moe_w_down[i].astype(BF16))
        h_b = mix(h_b)
        h_s = mix(h_s)

        with_meta = lambda real, meta: jnp.concatenate(
            [jnp.broadcast_to(meta[None], (B,) + meta.shape), real.reshape(B, S, -1)], axis=1)
        outs["ckv_p"].append(with_meta(ckv_b, ckv_s[ns:]))
        outs["kr_p"].append(with_meta(krs_b[:, 64:64 + rope_m], kr_s_compact[ns:]))
        outs["hg_p"].append(_unblock_t(sT_hg_b, hg_heads).astype(state_hgrn.dtype))
        outs["rt_p"].append(_unblock_t(sT_rt_b, rt_heads).astype(state_ret.dtype))
        outs["ckv_s"].append(ckv_s[:ns].reshape(Bd, Sd, nkv))
        outs["kr_s"].append(kr_s_compact[:ns].reshape(Bd, Sd, rope_m))
        outs["hg_s"].append(_unblock_t(sT_hg_s, hg_heads).astype(state_hgrn.dtype))
        outs["rt_s"].append(_unblock_t(sT_rt_s, rt_heads).astype(state_ret.dtype))

    y_prompt = h_b.reshape(B, S, D)
    y_sample = h_s[:ns].reshape(Bd, Sd, D)
    stack = lambda k: jnp.stack(outs[k])
    return (y_prompt, y_sample, stack("ckv_p"), stack("kr_p"), stack("hg_p"), stack("rt_p"),
            stack("ckv_s"), stack("kr_s"), stack("hg_s"), stack("rt_s"))
```

```python
import functools

import numpy as np
import jax
import jax.numpy as jnp
from jax import lax
from jax.experimental import pallas as pl
from jax.experimental.pallas import tpu as pltpu

F32 = jnp.float32
BF16 = jnp.bfloat16

EPS = 1e-6
ROPE_BASE = 10000.0
NEG_BIG = -1e30
LOG2E = 1.4426950408889634
LANES = 128
SUBLANES = 8
VMEM_LIMIT = 56 * 1024 * 1024

HG_CHUNK = 64
RET_CHUNK = 256
SMALL_CHUNK = 16
REC_BATCH_BLOCK = 8
PAGES_PER_STEP = 16
SCORE_BUFFERS = 3
FLASH_TILE = 1024


def _params(*sem):
    return pltpu.CompilerParams(dimension_semantics=sem, vmem_limit_bytes=VMEM_LIMIT)


def _row_tile(n, pref):
    if n <= pref:
        return n
    t = pref - pref % 16
    while t >= 16:
        if n % t == 0:
            return t
        t -= 16
    return n


def _dot(a, b):
    return jnp.dot(a.astype(BF16), b.astype(BF16), preferred_element_type=F32)


def _dot_nt(a, b):
    return lax.dot_general(a.astype(BF16), b.astype(BF16), (((1,), (1,)), ((), ())), preferred_element_type=F32)


def _dot_tn(a, b):
    return lax.dot_general(a.astype(BF16), b.astype(BF16), (((0,), (0,)), ((), ())), preferred_element_type=F32)


def _rms(x, width):
    return x * lax.rsqrt(jnp.sum(x * x, axis=-1, keepdims=True) * (1.0 / width) + EPS)


def _rope128(y, cos, sin_signed, half):
    lane = lax.broadcasted_iota(jnp.int32, y.shape, 1)
    first = (lane % (2 * half)) < half
    rot = jnp.where(first, pltpu.roll(y, LANES - half, 1), pltpu.roll(y, half, 1))
    return y * cos + rot * sin_signed


def _sigmoid(x):
    return 1.0 / (1.0 + jnp.exp(-x))


def _inproj_kernel(h_ref, gn_ref, w_ref, gq_ref, gkv_ref, gkr_ref, lb_ref, wq_ref, gqs_ref, gks_ref,
                   cm_ref, sm_ref, cr_ref, sr_ref, *out_refs, nq, nkv, rope_m, dh, nh, nope, scale):
    q_refs = out_refs[:-4]
    ckv_ref, kr_ref, hg_ref, rt_ref = out_refs[-4:]
    x = h_ref[...]
    xb = (_rms(x, x.shape[-1]) * gn_ref[...]).astype(BF16)

    def proj(lo, width):
        return jnp.dot(xb, w_ref[:, lo:lo + width], preferred_element_type=F32)

    cm = cm_ref[...]
    sm = sm_ref[...]
    c = proj(0, nq)
    cq = (_rms(c, nq) * gq_ref[...]).astype(BF16)
    gqs = gqs_ref[...]
    for h in range(nh):
        x = jnp.dot(cq, wq_ref[:, h * LANES:(h + 1) * LANES], preferred_element_type=F32)
        lane = lax.broadcasted_iota(jnp.int32, x.shape, 1)
        is_nope = lane < nope
        x2 = x * x
        ss_n = jnp.sum(jnp.where(is_nope, x2, 0.0), axis=-1, keepdims=True)
        ss_r = jnp.sum(jnp.where(is_nope, 0.0, x2), axis=-1, keepdims=True)
        inv = jnp.where(is_nope, lax.rsqrt(ss_n * (1.0 / nope) + EPS), lax.rsqrt(ss_r * (1.0 / rope_m) + EPS))
        y = _rope128(x * inv * gqs, cm, sm, rope_m // 2) * scale
        q_refs[0][:, h * LANES:(h + 1) * LANES] = y.astype(BF16)
        if len(q_refs) > 1:
            q_refs[1][:, h * LANES:(h + 1) * LANES] = (y * gks_ref[...]).astype(BF16)

    c = proj(nq, nkv)
    ckv_ref[...] = _rms(c, nkv) * gkv_ref[...]
    k = proj(nq + nkv, LANES)
    k = _rms(k, rope_m) * gkr_ref[...]
    kr_ref[...] = _rope128(k, cm, sm, rope_m // 2)

    base = nq + nkv + LANES
    hq = proj(base, dh)
    hg_ref[:, 0:dh] = hq * _sigmoid(hq)
    z = proj(base + dh, dh)
    lb = lb_ref[...]
    f = lb + (1.0 - lb) * _sigmoid(z)
    hg_ref[:, dh:2 * dh] = 1.0 - f
    hg_ref[:, 2 * dh:3 * dh] = jnp.log(f)
    hg_ref[:, 3 * dh:4 * dh] = proj(base + 2 * dh, dh)
    hg_ref[:, 4 * dh:5 * dh] = proj(base + 3 * dh, dh)

    base = base + 4 * dh
    cr = cr_ref[...]
    sr = sr_ref[...]
    for part, scale in ((0, 1.0), (1, 64 ** -0.5)):
        for s in range(dh // LANES):
            lo = part * dh + s * LANES
            y = _rope128(proj(base + lo, LANES), cr, sr, 32)
            rt_ref[:, lo:lo + LANES] = y * scale
    rt_ref[:, 2 * dh:3 * dh] = proj(base + 2 * dh, dh)
    rt_ref[:, 3 * dh:4 * dh] = proj(base + 3 * dh, dh)


def _inproj(h, gn, w, gq, gkv, gkr, lb, wq, gqs, gks, tabs, *, nq, nkv, rope_m, dh, nh, nope, scale, with_qg):
    n, d = h.shape
    tm = _row_tile(n, 512)
    cm, sm, cr, sr = tabs
    assert cm.shape[0] % tm == 0
    period = cm.shape[0] // tm
    row = lambda width: pl.BlockSpec((tm, width), lambda i: (i, 0))
    tab = pl.BlockSpec((tm, LANES), lambda i: (i % period, 0))
    const = lambda a: pl.BlockSpec(a.shape, lambda i: (0, 0))
    nqo = 2 if with_qg else 1
    return pl.pallas_call(
        functools.partial(_inproj_kernel, nq=nq, nkv=nkv, rope_m=rope_m, dh=dh, nh=nh, nope=nope, scale=scale),
        grid=(n // tm,),
        in_specs=[row(d), const(gn), const(w), const(gq), const(gkv), const(gkr), const(lb), const(wq), const(gqs),
                  const(gks), tab, tab, tab, tab],
        out_specs=[row(nh * LANES)] * nqo + [row(nkv), row(LANES), row(5 * dh), row(4 * dh)],
        out_shape=[jax.ShapeDtypeStruct((n, nh * LANES), BF16)] * nqo
                  + [jax.ShapeDtypeStruct((n, nkv), F32), jax.ShapeDtypeStruct((n, LANES), F32),
                     jax.ShapeDtypeStruct((n, 5 * dh), F32), jax.ShapeDtypeStruct((n, 4 * dh), F32)],
        compiler_params=_params("parallel"),
        name="inproj",
    )(h, gn, w, gq, gkv, gkr, lb, wq, gqs, gks, cm, sm, cr, sr)


def _kvup_kernel(c_ref, kr_ref, wk_ref, wvt_ref, gk_ref, k_ref, vt_ref, *, nh, nope):
    c = c_ref[...].astype(BF16)
    kr = kr_ref[...]
    gk = gk_ref[...]
    for h in range(nh):
        x = jnp.dot(c, wk_ref[:, h * LANES:(h + 1) * LANES], preferred_element_type=F32)
        k_ref[:, h * LANES:(h + 1) * LANES] = (_rms(x, nope) * gk + kr).astype(BF16)
    vt_ref[...] = _dot_nt(wvt_ref[...], c).astype(BF16)


def _kvup(ckv, krs, wk, wvt, gk, *, nh, nope):
    n, r = ckv.shape
    tm = _row_tile(n, 512)
    row = lambda width: pl.BlockSpec((tm, width), lambda i: (i, 0))
    const = lambda a: pl.BlockSpec(a.shape, lambda i: (0, 0))
    nv = wvt.shape[0]
    return pl.pallas_call(
        functools.partial(_kvup_kernel, nh=nh, nope=nope),
        grid=(n // tm,),
        in_specs=[row(r), row(LANES), const(wk), const(wvt), const(gk)],
        out_specs=[row(nh * LANES), pl.BlockSpec((nv, tm), lambda i: (0, i))],
        out_shape=[jax.ShapeDtypeStruct((n, nh * LANES), BF16), jax.ShapeDtypeStruct((nv, n), BF16)],
        compiler_params=_params("parallel"),
        name="kvup",
    )(ckv, krs, wk, wvt, gk)


def _flash_kernel(*refs, nh, vdim, has_prefix):
    if has_prefix:
        q_ref, k_ref, vt_ref, kp_ref, vtp_ref, o_ref, m_sc, l_sc, acc_sc, s_sc = refs
    else:
        q_ref, k_ref, vt_ref, o_ref, m_sc, l_sc, acc_sc, s_sc = refs
    qi = pl.program_id(1)
    ki = pl.program_id(2)

    def update(k_r, vt_r, masked):
        tk = k_r.shape[0]

        def scores(h):
            st = _dot_nt(k_r[:, h * LANES:(h + 1) * LANES], q_ref[:, h * LANES:(h + 1) * LANES])
            if masked:
                key = lax.broadcasted_iota(jnp.int32, st.shape, 0)
                qry = lax.broadcasted_iota(jnp.int32, st.shape, 1)
                st = jnp.where(key <= qry, st, NEG_BIG)
            s_sc[h % nbuf, 0:tk, :] = st

        nbuf = s_sc.shape[0]
        for h in range(nbuf - 1):
            scores(h)
        for h in range(nh):
            if h + nbuf - 1 < nh:
                scores(h + nbuf - 1)
            rows = slice(h * vdim, (h + 1) * vdim)
            st = s_sc[h % nbuf, 0:tk, :]
            m_prev = m_sc[h:h + 1, :]
            m_new = jnp.maximum(m_prev, jnp.max(st, axis=0, keepdims=True))
            alpha = jnp.exp2(m_prev - m_new)
            e = jnp.exp2(st - m_new)
            l_sc[h:h + 1, :] = alpha * l_sc[h:h + 1, :] + jnp.sum(e, axis=0, keepdims=True)
            m_sc[h:h + 1, :] = m_new
            pv = jnp.dot(vt_r[rows, :], e.astype(BF16), preferred_element_type=F32)
            acc_sc[rows, :] = acc_sc[rows, :] * alpha + pv

    @pl.when(ki == 0)
    def _():
        m_sc[...] = jnp.full(m_sc.shape, NEG_BIG, F32)
        l_sc[...] = jnp.zeros(l_sc.shape, F32)
        acc_sc[...] = jnp.zeros(acc_sc.shape, F32)
        if has_prefix:
            update(kp_ref, vtp_ref, False)

    @pl.when(ki < qi)
    def _():
        update(k_ref, vt_ref, False)

    @pl.when(ki == qi)
    def _():
        update(k_ref, vt_ref, True)
        for h in range(nh):
            rows = slice(h * vdim, (h + 1) * vdim)
            acc_sc[rows, :] = acc_sc[rows, :] / l_sc[h:h + 1, :]
        o_ref[...] = acc_sc[...].T


def _flash(q, k, vt, prefix, *, nh, batch):
    n = q.shape[0]
    t = n // batch
    dv = vt.shape[0]
    tq = _row_tile(t, FLASH_TILE)
    nt = t // tq
    qspec = pl.BlockSpec((tq, nh * LANES), lambda bi, qi, ki: (bi * nt + qi, 0))
    kspec = pl.BlockSpec((tq, nh * LANES), lambda bi, qi, ki: (bi * nt + jnp.minimum(ki, qi), 0))
    vspec = pl.BlockSpec((dv, tq), lambda bi, qi, ki: (0, bi * nt + jnp.minimum(ki, qi)))
    ins, specs = [q, k, vt], [qspec, kspec, vspec]
    if prefix is not None:
        for a in prefix:
            ins.append(a)
            specs.append(pl.BlockSpec(a.shape, lambda bi, qi, ki: (0, 0)))
    return pl.pallas_call(
        functools.partial(_flash_kernel, nh=nh, vdim=dv // nh, has_prefix=prefix is not None),
        grid=(batch, nt, nt),
        in_specs=specs,
        out_specs=pl.BlockSpec((tq, dv), lambda bi, qi, ki: (bi * nt + qi, 0)),
        out_shape=jax.ShapeDtypeStruct((n, dv), F32),
        scratch_shapes=[pltpu.VMEM((nh, tq), F32), pltpu.VMEM((nh, tq), F32), pltpu.VMEM((dv, tq), F32),
                        pltpu.VMEM((SCORE_BUFFERS, tq, tq), F32)],
        compiler_params=_params("parallel", "parallel", "arbitrary"),
        name="flash_prefix" if prefix is not None else "flash",
    )(*ins)


def _matmul_kernel(a_ref, w_ref, o_ref):
    o_ref[...] = jnp.dot(a_ref[...].astype(BF16), w_ref[...], preferred_element_type=F32).astype(o_ref.dtype)


def _matmul(a, w, out_dtype, name):
    n, k = a.shape
    tm = _row_tile(n, 512)
    return pl.pallas_call(
        _matmul_kernel,
        grid=(n // tm,),
        in_specs=[pl.BlockSpec((tm, k), lambda i: (i, 0)), pl.BlockSpec(w.shape, lambda i: (0, 0))],
        out_specs=pl.BlockSpec((tm, w.shape[1]), lambda i: (i, 0)),
        out_shape=jax.ShapeDtypeStruct((n, w.shape[1]), out_dtype),
        compiler_params=_params("parallel"),
        name=name,
    )(a, w)


def _paged_kernel(pt_ref, q_ref, wabs_ref, wukt_ref, sel_ref, cn_ref, krn_ref, ckv_hbm, krt_hbm, o_ref,
                  lhs_sc, cbuf, kbuf, sem, *, layer, npg, nchunk, sub, nh, nope, sd):
    b = pl.program_id(0)
    nb = pl.num_programs(0)
    nk = nh * nope
    rows = sd * nh
    page = ckv_hbm.shape[2]
    nslot = cbuf.shape[0]

    def chunk_slot(off):
        return lax.rem(b * nchunk + off, nslot)

    def page_copies(off, p):
        slot = chunk_slot(off)
        idx = pt_ref[b + off // nchunk, (off % nchunk) * npg + p]
        span = pl.ds(p * page, page)
        return (pltpu.make_async_copy(ckv_hbm.at[layer, idx], cbuf.at[slot, span, :], sem.at[0, slot]),
                pltpu.make_async_copy(krt_hbm.at[layer, idx], kbuf.at[slot, :, span], sem.at[1, slot]))

    def start_chunk(off):
        for p in range(npg):
            for cp in page_copies(off, p):
                cp.start()

    def wait_chunk(off):
        for p in range(npg):
            for cp in page_copies(off, p):
                cp.wait()

    @pl.when(b == 0)
    def _():
        start_chunk(0)
        start_chunk(1)

    lhs_sc[0:nk, :] = wukt_ref[...]
    lhs_sc[nk:nk + rows, :] = wabs_ref[0]
    qr = jnp.dot(q_ref[0], sel_ref[...], preferred_element_type=F32).astype(BF16)

    def scores(c, krt):
        big = _dot_nt(lhs_sc[...], c)
        kraw = big[0:nk]
        ssq = jnp.sum((kraw * kraw).reshape(nh, nope, kraw.shape[-1]), axis=1)
        r = lax.rsqrt(ssq * (1.0 / nope) + EPS)
        return big[nk:nk + rows] * jnp.concatenate([r] * sd, axis=0) + jnp.dot(qr, krt, preferred_element_type=F32)

    def update(state, s, c):
        m_prev, l_prev, acc = state
        m_new = jnp.maximum(m_prev, jnp.max(s, axis=-1, keepdims=True))
        alpha = jnp.exp2(m_prev - m_new)
        e = jnp.exp2(s - m_new)
        l_new = alpha * l_prev + jnp.sum(e, axis=-1, keepdims=True)
        return m_new, l_new, acc * alpha + jnp.dot(e.astype(BF16), c, preferred_element_type=F32)

    def chunk_scores(off):
        slot = chunk_slot(off)
        span = sub * page
        cs, ss = [], []
        for g in range(npg // sub):
            lat = cbuf[slot, g * span:(g + 1) * span, :].astype(BF16)
            cs.append(lat)
            ss.append(scores(lat, kbuf[slot, :, g * span:(g + 1) * span].astype(BF16)))
        return jnp.concatenate(ss, axis=1), jnp.concatenate(cs, axis=0)

    state = (jnp.full((rows, 1), NEG_BIG, F32), jnp.zeros((rows, 1), F32), jnp.zeros((rows, cbuf.shape[2]), F32))
    wait_chunk(0)
    cur = chunk_scores(0)
    for c in range(nchunk):
        if c + 2 < nchunk:
            start_chunk(c + 2)
        else:
            @pl.when(b + 1 < nb)
            def _():
                start_chunk(c + 2)
        if c + 1 < nchunk:
            wait_chunk(c + 1)
            nxt = chunk_scores(c + 1)
        state = update(state, *cur)
        if c + 1 < nchunk:
            cur = nxt

    lat = cn_ref[0].astype(BF16)
    s = scores(lat, krn_ref[0].astype(BF16))
    row = lax.broadcasted_iota(jnp.int32, s.shape, 0)
    col = lax.broadcasted_iota(jnp.int32, s.shape, 1)
    _, l_fin, acc = update(state, jnp.where(col * nh <= row, s, NEG_BIG), lat)
    o_ref[0] = acc / l_fin


def _paged_attention(page_table, q, wabs, wukt, sel, cn, krnt, cache_ckv, cache_kropet, layer, *, nh, nope, sd):
    bd, rows, _ = q.shape
    n_pages = page_table.shape[1]
    assert n_pages % 2 == 0
    npg = max(p for p in range(1, PAGES_PER_STEP + 1) if n_pages % (2 * p) == 0)
    nchunk = n_pages // npg
    nslot = 3
    sub = 2 if npg % 2 == 0 else 1
    page, kvl = cache_ckv.shape[2:]
    rp = cache_kropet.shape[2]
    per_b = lambda a: pl.BlockSpec((1,) + a.shape[1:], lambda b, pt: (b, 0, 0))
    const = lambda a: pl.BlockSpec(a.shape, lambda b, pt: (0, 0))
    hbm = pl.BlockSpec(memory_space=pl.ANY)
    grid_spec = pltpu.PrefetchScalarGridSpec(
        num_scalar_prefetch=1,
        grid=(bd,),
        in_specs=[per_b(q), per_b(wabs), const(wukt), const(sel), per_b(cn), per_b(krnt), hbm, hbm],
        out_specs=pl.BlockSpec((1, rows, kvl), lambda b, pt: (b, 0, 0)),
        scratch_shapes=[pltpu.VMEM((nh * nope + rows, kvl), BF16), pltpu.VMEM((nslot, npg * page, kvl), F32),
                        pltpu.VMEM((nslot, rp, npg * page), F32), pltpu.SemaphoreType.DMA((2, nslot))],
    )
    return pl.pallas_call(
        functools.partial(_paged_kernel, layer=layer, npg=npg, nchunk=nchunk, sub=sub, nh=nh, nope=nope, sd=sd),
        grid_spec=grid_spec,
        out_shape=jax.ShapeDtypeStruct((bd, rows, kvl), F32),
        compiler_params=_params("arbitrary"),
        name="paged_attention",
    )(page_table, q, wabs, wukt, sel, cn, krnt, cache_ckv, cache_kropet)


def _block_mask(n, blk):
    r = lax.broadcasted_iota(jnp.int32, (n, n), 0) // blk
    c = lax.broadcasted_iota(jnp.int32, (n, n), 1) // blk
    return r == c


def _expand_state(c, dh, dk):
    return jnp.where(_block_mask(dh, dk), jnp.concatenate([c] * (dh // dk), axis=0), 0.0)


def _collapse_state(st, dk):
    out = st[0:dk, :]
    for h in range(1, st.shape[0] // dk):
        out = out + st[h * dk:(h + 1) * dk, :]
    return out


def _split3(x):
    hi = x.astype(BF16)
    r1 = x - hi.astype(F32)
    mid = r1.astype(BF16)
    lo = (r1 - mid.astype(F32)).astype(BF16)
    return hi, mid, lo


def _hgrn_kernel(x_ref, s0_ref, o_ref, sT_ref, st_sc, pad_sc, sh_sc, e_sc, *, dh, dk, chunk, bb):
    for i in range(bb):
        _hgrn_one(x_ref.at[i], s0_ref.at[i % s0_ref.shape[0]], o_ref.at[i], sT_ref.at[i],
                  st_sc.at[i], pad_sc.at[i], sh_sc.at[i], e_sc.at[i], dh=dh, dk=dk, chunk=chunk)


def _hgrn_one(x_ref, s0_ref, o_ref, sT_ref, st_sc, pad_sc, sh_sc, e_sc, *, dh, dk, chunk):
    ci = pl.program_id(1)

    @pl.when(ci == 0)
    def _():
        st_sc[...] = _expand_state(s0_ref[...], dh, dk)
        pad_sc[:, 0:SUBLANES, :] = jnp.zeros((3, SUBLANES, dh), F32)

    q = x_ref[:, 0:dh]
    k = x_ref[:, dh:2 * dh]
    g = x_ref[:, 2 * dh:3 * dh]
    v = x_ref[:, 3 * dh:4 * dh]
    r = lax.broadcasted_iota(jnp.int32, (chunk, chunk), 0)
    c = lax.broadcasted_iota(jnp.int32, (chunk, chunk), 1)
    tri = jnp.where(c <= r, 1.0, 0.0).astype(BF16)
    b = sum(jnp.dot(tri, part, preferred_element_type=F32) for part in _split3(g * LOG2E))
    for i, val in enumerate((k, b, v)):
        pad_sc[i, SUBLANES:SUBLANES + chunk, :] = val
        sh_sc[i, 0] = val
        for r in range(1, SUBLANES):
            sh_sc[i, r] = pad_sc[i, SUBLANES - r:SUBLANES - r + chunk, :]
    ones_bd = jnp.where(_block_mask(dh, dk), 1.0, 0.0).astype(BF16)

    o_ref[...] = _dot_nt(q * jnp.exp2(b), st_sc[...])
    bands = []
    off = 0
    for a in range(chunk // SUBLANES):
        lo = a * SUBLANES
        rows = chunk - lo
        qa = q[lo:, :]
        ba = b[lo:, :]
        for r in range(SUBLANES):
            e_sc[off:off + rows, :] = qa * sh_sc[0, r, 0:rows, :] * jnp.exp2(ba - sh_sc[1, r, 0:rows, :])
            bands.append((lo, rows, r, off))
            off += rows
    e_sc[...] = jnp.dot(e_sc[...].astype(BF16), ones_bd, preferred_element_type=F32)
    for a in range(chunk // SUBLANES):
        acc = None
        for lo, rows, r, off in bands[a * SUBLANES:(a + 1) * SUBLANES]:
            term = e_sc[off:off + rows, :] * sh_sc[2, r, 0:rows, :]
            acc = term if acc is None else acc + term
        o_ref[lo:, :] += acc

    b_last = b[chunk - 1:chunk, :]
    upd = _dot_tn(v, k * jnp.exp2(b_last - b))
    st_sc[...] = st_sc[...] * jnp.exp2(b_last) + jnp.where(_block_mask(dh, dk), upd, 0.0)

    @pl.when(ci == pl.num_programs(1) - 1)
    def _():
        sT_ref[...] = _collapse_state(st_sc[...], dk)


def _ret_kernel(x_ref, s0_ref, o_ref, sT_ref, st_sc, *, bb, **kw):
    for i in range(bb):
        _ret_one(x_ref.at[i], s0_ref.at[i % s0_ref.shape[0]], o_ref.at[i], sT_ref.at[i], st_sc.at[i], **kw)


def _ret_one(x_ref, s0_ref, o_ref, sT_ref, st_sc, *, dh, dk, chunk, n_valid, log_gamma):
    ci = pl.program_id(1)

    @pl.when(ci == 0)
    def _():
        st_sc[...] = _expand_state(s0_ref[...], dh, dk)

    q = x_ref[:, 0:dh]
    k = x_ref[:, dh:2 * dh]
    v = x_ref[:, 2 * dh:3 * dh]
    lane = lax.broadcasted_iota(jnp.int32, (chunk, dh), 1)
    head = lane // dk
    lg = jnp.zeros((chunk, dh), F32)
    for h, val in enumerate(log_gamma):
        lg = jnp.where(head == h, val, lg)
    t = lax.broadcasted_iota(jnp.int32, (chunk, dh), 0).astype(F32)
    r = lax.broadcasted_iota(jnp.int32, (chunk, chunk), 0)
    c = lax.broadcasted_iota(jnp.int32, (chunk, chunk), 1)
    diff = (r - c).astype(F32)

    o = _dot_nt(q, st_sc[...]) * jnp.exp((t + 1.0) * lg)
    for h, val in enumerate(log_gamma):
        in_h = head == h
        a = _dot_nt(jnp.where(in_h, q, 0.0), k)
        a = a * jnp.where(diff >= 0, jnp.exp(jnp.maximum(diff, 0.0) * val), 0.0)
        o = o + jnp.where(in_h, _dot(a, v), 0.0)
    o_ref[...] = o

    upd = _dot_tn(v, k * jnp.exp((n_valid - 1.0 - t) * lg))
    st_sc[...] = st_sc[...] * jnp.exp(n_valid * lg[0:1, :]) + jnp.where(_block_mask(dh, dk), upd, 0.0)

    @pl.when(ci == pl.num_programs(1) - 1)
    def _():
        sT_ref[...] = _collapse_state(st_sc[...], dk)


def _recurrence(kind, x, s0, *, dh, dk, chunk, shared_state):
    b, t, w = x.shape
    n_valid = min(chunk, t)
    if t < chunk:
        x = jnp.pad(x, ((0, 0), (0, chunk - t), (0, 0)))
    t_pad = x.shape[1]
    bb = max(d for d in range(1, REC_BATCH_BLOCK + 1) if b % d == 0)
    s0_spec = (pl.BlockSpec((1, dk, dh), lambda bi, ci: (0, 0, 0)) if shared_state
               else pl.BlockSpec((bb, dk, dh), lambda bi, ci: (bi, 0, 0)))
    if kind == "hgrn":
        body = functools.partial(_hgrn_kernel, dh=dh, dk=dk, chunk=chunk, bb=bb)
        band_rows = SUBLANES * sum(range(SUBLANES, chunk + 1, SUBLANES))
        scratch = [pltpu.VMEM((bb, dh, dh), F32), pltpu.VMEM((bb, 3, SUBLANES + chunk, dh), F32),
                   pltpu.VMEM((bb, 3, SUBLANES, chunk, dh), F32), pltpu.VMEM((bb, band_rows, dh), F32)]
    else:
        log_gamma = tuple(float(np.log1p(-np.exp2(-5.0 - h))) for h in range(dh // dk))
        body = functools.partial(_ret_kernel, bb=bb, dh=dh, dk=dk, chunk=chunk, n_valid=n_valid,
                                 log_gamma=log_gamma)
        scratch = [pltpu.VMEM((bb, dh, dh), F32)]
    return pl.pallas_call(
        body,
        grid=(b // bb, t_pad // chunk),
        in_specs=[pl.BlockSpec((bb, chunk, w), lambda bi, ci: (bi, ci, 0)), s0_spec],
        out_specs=[pl.BlockSpec((bb, chunk, dh), lambda bi, ci: (bi, ci, 0)),
                   pl.BlockSpec((bb, dk, dh), lambda bi, ci: (bi, 0, 0))],
        out_shape=[jax.ShapeDtypeStruct((b, t_pad, dh), F32), jax.ShapeDtypeStruct((b, dk, dh), F32)],
        scratch_shapes=scratch,
        compiler_params=_params("parallel", "arbitrary"),
        name=kind,
    )(x, s0)


def _merge_kernel(om_ref, oh_ref, hg_ref, or_ref, rg_ref, h_ref, w_ref, gh_ref, gr_ref, o_ref, *, nm, dh, dv):
    def gated(o_r, gate_r, g_r):
        o = o_r[...]
        gate = gate_r[...]
        out = []
        for s in range(dh // LANES):
            x = o[:, s * LANES:(s + 1) * LANES]
            lane = lax.broadcasted_iota(jnp.int32, x.shape, 1)
            low = lane < dv
            x2 = x * x
            ss_l = jnp.sum(jnp.where(low, x2, 0.0), axis=-1, keepdims=True)
            ss_h = jnp.sum(jnp.where(low, 0.0, x2), axis=-1, keepdims=True)
            inv = lax.rsqrt(jnp.where(low, ss_l, ss_h) * (1.0 / dv) + EPS)
            gt = gate[:, s * LANES:(s + 1) * LANES]
            out.append(x * inv * g_r[...] * (gt * _sigmoid(gt)))
        return jnp.concatenate(out, axis=-1)

    y = _dot(om_ref[...], w_ref[0:nm, :])
    y = y + _dot(gated(oh_ref, hg_ref, gh_ref), w_ref[nm:nm + dh, :])
    y = y + _dot(gated(or_ref, rg_ref, gr_ref), w_ref[nm + dh:nm + 2 * dh, :])
    o_ref[...] = h_ref[...] + y


def _merge(o_mla, o_hg, hg_pack, o_rt, rt_pack, h, w_o, gh, gr, *, dh, dv):
    n, d = h.shape
    nm = o_mla.shape[1]
    tm = _row_tile(n, 512)
    row = lambda width, cb=0: pl.BlockSpec((tm, width), lambda i: (i, cb))
    const = lambda a: pl.BlockSpec(a.shape, lambda i: (0, 0))
    return pl.pallas_call(
        functools.partial(_merge_kernel, nm=nm, dh=dh, dv=dv),
        grid=(n // tm,),
        in_specs=[row(nm), row(dh), row(dh, 4), row(dh), row(dh, 3), row(d), const(w_o), const(gh), const(gr)],
        out_specs=row(d),
        out_shape=jax.ShapeDtypeStruct((n, d), F32),
        compiler_params=_params("parallel"),
        name="merge",
    )(o_mla, o_hg, hg_pack, o_rt, rt_pack, h, w_o, gh, gr)


def _ffn_kernel(h_ref, g_ref, wg_ref, wu_ref, wd_ref, o_ref, xn_sc, acc_sc):
    j = pl.program_id(1)

    @pl.when(j == 0)
    def _():
        x = h_ref[...]
        xn_sc[...] = (_rms(x, x.shape[-1]) * g_ref[...]).astype(BF16)
        acc_sc[...] = jnp.zeros(acc_sc.shape, F32)

    xn = xn_sc[...]
    a = jnp.dot(xn, wg_ref[...], preferred_element_type=F32)
    u = jnp.dot(xn, wu_ref[...], preferred_element_type=F32)
    acc_sc[...] += jnp.dot((a * _sigmoid(a) * u).astype(BF16), wd_ref[...], preferred_element_type=F32)

    @pl.when(j == pl.num_programs(1) - 1)
    def _():
        o_ref[...] = h_ref[...] + acc_sc[...]


def _ffn(h, g, wg, wu, wd):
    n, d = h.shape
    ff = wg.shape[1]
    tm = _row_tile(n, 512)
    tf = ff // 2 if (ff // 2) % LANES == 0 else ff
    return pl.pallas_call(
        _ffn_kernel,
        grid=(n // tm, ff // tf),
        in_specs=[pl.BlockSpec((tm, d), lambda i, j: (i, 0)), pl.BlockSpec(g.shape, lambda i, j: (0, 0)),
                  pl.BlockSpec((d, tf), lambda i, j: (0, j)), pl.BlockSpec((d, tf), lambda i, j: (0, j)),
                  pl.BlockSpec((tf, d), lambda i, j: (j, 0))],
        out_specs=pl.BlockSpec((tm, d), lambda i, j: (i, 0)),
        out_shape=jax.ShapeDtypeStruct((n, d), F32),
        scratch_shapes=[pltpu.VMEM((tm, d), BF16), pltpu.VMEM((tm, d), F32)],
        compiler_params=_params("parallel", "arbitrary"),
        name="ffn",
    )(h, g, wg, wu, wd)


def _moe_kernel(h_ref, g_ref, wr_ref, wg_ref, wu_ref, wd_ref, o_ref, xn_sc, gate_sc, acc_sc, *, ne):
    e = pl.program_id(1)

    @pl.when(e == 0)
    def _():
        x = h_ref[...]
        xn = _rms(x, x.shape[-1]) * g_ref[...]
        xn_sc[...] = xn.astype(BF16)
        acc_sc[...] = jnp.zeros(acc_sc.shape, F32)
        xh, xm, _ = _split3(xn)
        wh, wm, _ = _split3(wr_ref[...])
        dd = lambda a, b: jnp.dot(a, b, preferred_element_type=F32)
        logits = dd(xh, wh) + (dd(xh, wm) + dd(xm, wh))
        lane = lax.broadcasted_iota(jnp.int32, logits.shape, 1)
        valid = lane < ne
        lg = jnp.where(valid, logits, -jnp.inf)
        v1 = jnp.max(lg, axis=-1, keepdims=True)
        i1 = jnp.min(jnp.where(lg == v1, lane, LANES), axis=-1, keepdims=True)
        lg2 = jnp.where(lane == i1, -jnp.inf, lg)
        v2 = jnp.max(lg2, axis=-1, keepdims=True)
        i2 = jnp.min(jnp.where(lg2 == v2, lane, LANES), axis=-1, keepdims=True)
        e2 = jnp.exp(v2 - v1)
        w1 = 1.0 / (1.0 + e2)
        w2 = e2 / (1.0 + e2)
        gate_sc[...] = jnp.where(lane == i1, w1, jnp.where(lane == i2, w2, 0.0))

    lane = lax.broadcasted_iota(jnp.int32, gate_sc.shape, 1)
    gate = jnp.sum(jnp.where(lane == e, gate_sc[...], 0.0), axis=-1, keepdims=True)
    xn = xn_sc[...]
    a = jnp.dot(xn, wg_ref[0], preferred_element_type=F32)
    u = jnp.dot(xn, wu_ref[0], preferred_element_type=F32)
    acc_sc[...] += gate * jnp.dot((a * _sigmoid(a) * u).astype(BF16), wd_ref[0], preferred_element_type=F32)

    @pl.when(e == ne - 1)
    def _():
        o_ref[...] = h_ref[...] + acc_sc[...]


def _moe(h, g, wr, wg, wu, wd):
    n, d = h.shape
    ne, _, ff = wg.shape
    tm = _row_tile(n, 512)
    return pl.pallas_call(
        functools.partial(_moe_kernel, ne=ne),
        grid=(n // tm, ne),
        in_specs=[pl.BlockSpec((tm, d), lambda i, e: (i, 0)), pl.BlockSpec(g.shape, lambda i, e: (0, 0)),
                  pl.BlockSpec(wr.shape, lambda i, e: (0, 0)),
                  pl.BlockSpec((1, d, ff), lambda i, e: (e, 0, 0)), pl.BlockSpec((1, d, ff), lambda i, e: (e, 0, 0)),
                  pl.BlockSpec((1, ff, d), lambda i, e: (e, 0, 0))],
        out_specs=pl.BlockSpec((tm, d), lambda i, e: (i, 0)),
        out_shape=jax.ShapeDtypeStruct((n, d), F32),
        scratch_shapes=[pltpu.VMEM((tm, d), BF16), pltpu.VMEM((tm, LANES), F32), pltpu.VMEM((tm, d), F32)],
        compiler_params=_params("parallel", "arbitrary"),
        name="moe",
    )(h, g, wr, wg, wu, wd)


def _rope_tables(pos, rope_m, ret_dk):
    def angles(d):
        inv = ROPE_BASE ** (-jnp.arange(0, d, 2, dtype=F32) / d)
        return pos.astype(F32)[:, None] * inv[None, :]

    n = pos.shape[0]
    a = angles(rope_m)
    c, s = jnp.cos(a), jnp.sin(a)
    pad = LANES - 64 - rope_m
    cm = jnp.concatenate([jnp.ones((n, 64), F32), c, c, jnp.ones((n, pad), F32)], axis=1)
    sm = jnp.concatenate([jnp.zeros((n, 64), F32), -s, s, jnp.zeros((n, pad), F32)], axis=1)
    a = angles(ret_dk)
    c, s = jnp.cos(a), jnp.sin(a)
    reps = LANES // ret_dk
    cr = jnp.concatenate([c, c] * reps, axis=1)
    sr = jnp.concatenate([-s, s] * reps, axis=1)
    return cm, sm, cr, sr


def _slab_cols(w, nope, rope_m):
    r, nh, dd = w.shape
    return jnp.pad(w, ((0, 0), (0, 0), (0, LANES - dd))).reshape(r, nh * LANES)


def _block_diag_t(s):
    b, nh, dk, dv = s.shape
    return s.transpose(0, 3, 1, 2).reshape(b, dv, nh * dk)


def _unblock_t(st, nh):
    b, dv, n = st.shape
    return st.reshape(b, dv, nh, n // nh).transpose(0, 2, 3, 1)


def kernel(x_prompt, x_sample, cache_ckv, cache_krope, page_table, state_hgrn, state_ret, meta_tokens,
           norm_mix_g, w_in, q_lora_g, kv_lora_g, w_uq, w_uk, w_uv, qn_nope_g, qn_rope_g, kn_nope_g, kn_rope_g,
           hg_lb, hg_norm_g, ret_norm_g, w_o, norm_ffn_g, ffn_w_gate, ffn_w_up, ffn_w_down,
           moe_router, moe_w_gate, moe_w_up, moe_w_down):
    depth = w_in.shape[0]
    B, S, D = x_prompt.shape
    Bd, Sd, _ = x_sample.shape
    n_meta = meta_tokens.shape[0]
    nq, nkv = q_lora_g.shape[1], kv_lora_g.shape[1]
    nh, nope, rope_m = w_uq.shape[2], qn_nope_g.shape[1], qn_rope_g.shape[1]
    vdim = w_uv.shape[3]
    hg_heads, hg_dk = state_hgrn.shape[2], state_hgrn.shape[3]
    rt_heads, rt_dk = state_ret.shape[2], state_ret.shape[3]
    dh = hg_heads * hg_dk
    assert nope == 64 and rope_m == 32 and vdim == 64 and hg_dk == 64 and rt_dk == 64
    assert dh == rt_heads * rt_dk and nh % 2 == 0
    past_len = page_table.shape[1] * cache_ckv.shape[2]
    scale = float((nope + rope_m) ** -0.5 * LOG2E)
    ns = Bd * Sd

    h_b = x_prompt.reshape(B * S, D)
    h_s = jnp.concatenate([x_sample.reshape(ns, D), meta_tokens.astype(x_prompt.dtype)], axis=0)
    pos_b = jnp.arange(S, dtype=F32) + n_meta
    pos_s = jnp.concatenate([jnp.tile(jnp.arange(Sd, dtype=F32) + past_len, Bd), jnp.arange(n_meta, dtype=F32)])
    tabs_b = _rope_tables(pos_b, rope_m, rt_dk)
    tabs_s = _rope_tables(pos_s, rope_m, rt_dk)

    lb_soft = jax.nn.softmax(hg_lb.astype(F32), axis=0)
    lb_all = jnp.cumsum(lb_soft, axis=0) - lb_soft[0]

    lane_pad = lambda g, lo: jnp.pad(g, (lo, LANES - lo - g.shape[0]))[None, :]
    sel = jnp.zeros((LANES, rope_m), BF16).at[jnp.arange(rope_m) + 64, jnp.arange(rope_m)].set(1.0)
    cache_kropet = jnp.swapaxes(cache_krope, 2, 3)

    outs = {k: [] for k in ("ckv_p", "kr_p", "hg_p", "rt_p", "ckv_s", "kr_s", "hg_s", "rt_s")}
    for l in range(depth):
        wi = w_in[l]
        o_kr = nq + nkv
        w_perm = jnp.concatenate(
            [wi[:, :o_kr], jnp.zeros((D, 64), F32), wi[:, o_kr:o_kr + rope_m], jnp.zeros((D, LANES - 64 - rope_m), F32),
             wi[:, o_kr + rope_m:]], axis=1).astype(BF16)
        wq = _slab_cols(w_uq[l], nope, rope_m).astype(BF16)
        wk = _slab_cols(w_uk[l], nope, rope_m).astype(BF16)
        wvt = w_uv[l].reshape(nkv, nh * vdim).T.astype(BF16)
        wukt = w_uk[l].reshape(nkv, nh * nope).T.astype(BF16)
        wabs_w = jnp.einsum('rhd,hg->hdgr', w_uk[l], jnp.eye(nh, dtype=F32))
        wabs_w = jnp.pad(wabs_w, ((0, 0), (0, LANES - nope), (0, 0), (0, 0))).reshape(nh * LANES, nh * nkv).astype(BF16)
        wuv_bd = jnp.einsum('rhd,hg->hrgd', w_uv[l], jnp.eye(nh, dtype=F32)).reshape(nh * nkv, nh * vdim).astype(BF16)
        g_q = (lane_pad(qn_nope_g[l], 0) + lane_pad(qn_rope_g[l], 64)).astype(F32)
        g_k = lane_pad(kn_nope_g[l], 0).astype(F32)
        g_kr = lane_pad(kn_rope_g[l], 64).astype(F32)
        g_hg = jnp.tile(hg_norm_g[l], LANES // hg_dk)[None, :]
        g_rt = jnp.tile(ret_norm_g[l], LANES // rt_dk)[None, :]
        wo = w_o[l].astype(BF16)
        inproj = functools.partial(_inproj, gn=norm_mix_g[l][None, :], w=w_perm, gq=q_lora_g[l][None, :],
                                   gkv=kv_lora_g[l][None, :], gkr=g_kr, lb=lb_all[l][None, :], wq=wq, gqs=g_q, gks=g_k,
                                   nq=nq, nkv=nkv, rope_m=rope_m, dh=dh, nh=nh, nope=nope, scale=scale)

        q_b, ckv_b, krs_b, hg_b, rt_b = inproj(h_b, tabs=tabs_b, with_qg=False)
        q_s, qg_s, ckv_s, krs_s, hg_s, rt_s = inproj(h_s, tabs=tabs_s, with_qg=True)
        k_b, vt_b = _kvup(ckv_b, krs_b, wk, wvt, g_k, nh=nh, nope=nope)
        k_m, vt_m = _kvup(ckv_s[ns:], krs_s[ns:], wk, wvt, g_k, nh=nh, nope=nope)

        o_b = _flash(q_b, k_b, vt_b, (k_m, vt_m), nh=nh, batch=B)
        mpad = LANES - n_meta
        o_m = _flash(jnp.pad(q_s[ns:], ((0, mpad), (0, 0))), jnp.pad(k_m, ((0, mpad), (0, 0))),
                     jnp.pad(vt_m, ((0, 0), (0, mpad))), None, nh=nh, batch=1)[:n_meta]
        wabs = _matmul(qg_s[:ns], wabs_w, BF16, "qabsorb").reshape(Bd, Sd * nh, nkv)
        pad_new = lambda a: jnp.pad(a.reshape(Bd, Sd, -1), ((0, 0), (0, LANES - Sd), (0, 0)))
        kr_s_compact = krs_s[:, 64:64 + rope_m]
        acc = _paged_attention(page_table, q_s[:ns].reshape(Bd, Sd * nh, LANES), wabs, wukt, sel,
                               pad_new(ckv_s[:ns]), pad_new(kr_s_compact[:ns]).swapaxes(1, 2), cache_ckv, cache_kropet, l,
                               nh=nh, nope=nope, sd=Sd)
        o_s = _matmul(acc.reshape(ns, nh * nkv), wuv_bd, F32, "vabsorb")
        o_small = jnp.concatenate([o_s, o_m], axis=0)

        zero_state = jnp.zeros((1, hg_dk, dh), F32)
        rec = functools.partial(_recurrence, dh=dh, dk=hg_dk)
        ohg_m, sT_hg_m = rec("hgrn", hg_s[ns:][None], zero_state, chunk=SMALL_CHUNK, shared_state=True)
        ohg_s, sT_hg_s = rec("hgrn", hg_s[:ns].reshape(Bd, Sd, -1), _block_diag_t(state_hgrn[l].astype(F32)),
                             chunk=SMALL_CHUNK, shared_state=False)
        ohg_b, sT_hg_b = rec("hgrn", hg_b.reshape(B, S, -1), sT_hg_m, chunk=HG_CHUNK, shared_state=True)
        ort_m, sT_rt_m = rec("ret", rt_s[ns:][None], zero_state, chunk=SMALL_CHUNK, shared_state=True)
        ort_s, sT_rt_s = rec("ret", rt_s[:ns].reshape(Bd, Sd, -1), _block_diag_t(state_ret[l].astype(F32)),
                             chunk=SMALL_CHUNK, shared_state=False)
        ort_b, sT_rt_b = rec("ret", rt_b.reshape(B, S, -1), sT_rt_m, chunk=RET_CHUNK, shared_state=True)
        ohg_small = jnp.concatenate([ohg_s[:, :Sd].reshape(ns, dh), ohg_m[0, :n_meta]], axis=0)
        ort_small = jnp.concatenate([ort_s[:, :Sd].reshape(ns, dh), ort_m[0, :n_meta]], axis=0)

        merge = functools.partial(_merge, w_o=wo, gh=g_hg, gr=g_rt, dh=dh, dv=hg_dk)
        h_b = merge(o_b, ohg_b.reshape(B * S, dh), hg_b, ort_b.reshape(B * S, dh), rt_b, h_b)
        h_s = merge(o_small, ohg_small, hg_s, ort_small, rt_s, h_s)
        g_ffn = norm_ffn_g[l][None, :]
        i = l // 2
        if l % 2 == 0:
            mix = functools.partial(_ffn, g=g_ffn, wg=ffn_w_gate[i].astype(BF16), wu=ffn_w_up[i].astype(BF16),
                                    wd=ffn_w_down[i].astype(BF16))
        else:
            ne = moe_router.shape[2]
            wr = jnp.pad(moe_router[i], ((0, 0), (0, LANES - ne)))
            mix = functools.partial(_moe, g=g_ffn, wr=wr, wg=moe_w_gate[i].astype(BF16), wu=moe_w_up[i].astype(BF16),
                                    wd=moe_w_down[i].astype(BF16))
        h_b = mix(h_b)
        h_s = mix(h_s)

        with_meta = lambda real, meta: jnp.concatenate(
            [jnp.broadcast_to(meta[None], (B,) + meta.shape), real.reshape(B, S, -1)], axis=1)
        outs["ckv_p"].append(with_meta(ckv_b, ckv_s[ns:]))
        outs["kr_p"].append(with_meta(krs_b[:, 64:64 + rope_m], kr_s_compact[ns:]))
        outs["hg_p"].append(_unblock_t(sT_hg_b, hg_heads).astype(state_hgrn.dtype))
        outs["rt_p"].append(_unblock_t(sT_rt_b, rt_heads).astype(state_ret.dtype))
        outs["ckv_s"].append(ckv_s[:ns].reshape(Bd, Sd, nkv))
        outs["kr_s"].append(kr_s_compact[:ns].reshape(Bd, Sd, rope_m))
        outs["hg_s"].append(_unblock_t(sT_hg_s, hg_heads).astype(state_hgrn.dtype))
        outs["rt_s"].append(_unblock_t(sT_rt_s, rt_heads).astype(state_ret.dtype))

    y_prompt = h_b.reshape(B, S, D)
    y_sample = h_s[:ns].reshape(Bd, Sd, D)
    stack = lambda k: jnp.stack(outs[k])
    return (y_prompt, y_sample, stack("ckv_p"), stack("kr_p"), stack("hg_p"), stack("rt_p"),
            stack("ckv_s"), stack("kr_s"), stack("hg_s"), stack("rt_s"))
```
